```python
import math
import jax
import jax.numpy as jnp
from jax import lax
import numpy as np

D_MODEL = 1024
BATCH = 4
SEQ = 8192
DEPTH = 2

CTX_LEN = 256
GRID_W = 64
N_ATT_LAYERS = (DEPTH + 1) // 2
N_REC_LAYERS = DEPTH // 2

NA_HEADS = 8
NA_HEAD_DIM = 64
NA_WIN_ROWS = 8
NA_WIN_COLS = 16
DIFF_HEADS = 4
DIFF_HEAD_DIM = 64
DIFF_BLOCK = 128
HGRN_HEADS = 4
HGRN_HEAD_DIM = 128
GDN_HEADS = 4
GDN_HEAD_DIM = 128
SHORT_CONV = 3
CHUNK = 64
D_FF = -(-8 * D_MODEL // (3 * 256)) * 256

NA_W = NA_HEADS * NA_HEAD_DIM
DF_W = DIFF_HEADS * 2 * DIFF_HEAD_DIM
ATT_IN = 3 * NA_W + 3 * DF_W
HG_W = HGRN_HEADS * HGRN_HEAD_DIM
GD_W = GDN_HEADS * GDN_HEAD_DIM
REC_IN = 5 * HG_W + 4 * GD_W + 4 * GDN_HEADS
MIX_W = NA_W + DF_W

ROPE_THETA = 10000.0
EPS = 1e-6
NEG_INF = -1e30

kernel_name = 'hybrid_diffusion_trunk'


def rmsnorm(x, w):
    xf = x.astype(jnp.float32)
    y = xf * lax.rsqrt(jnp.mean(xf * xf, axis=-1, keepdims=True) + EPS)
    return (y * w.astype(jnp.float32)).astype(x.dtype)


def l2norm(t):
    return t * lax.rsqrt(jnp.sum(t * t, axis=-1, keepdims=True) + EPS)


def to_heads(t, n_heads):
    b, s, _ = t.shape
    return t.reshape(b, s, n_heads, -1).transpose(0, 2, 1, 3)


def from_heads(t):
    b, h, s, d = t.shape
    return t.transpose(0, 2, 1, 3).reshape(b, s, h * d)


def adaln(cvec, w, b):
    return jnp.split(jax.nn.silu(cvec) @ w + b, 6, axis=-1)


def modulate(h, shift, scale):
    return h * (1 + scale) + shift


def swiglu(h, w_gate, w_up, w_down):
    return (jax.nn.silu(h @ w_gate) * (h @ w_up)) @ w_down


def axial_rope(x, rows, cols):
    half = x.shape[-1] // 2
    quarter = half // 2
    inv_freq = ROPE_THETA ** (-jnp.arange(quarter, dtype=jnp.float32) / quarter)

    def rotate(xp, pos):
        ang = pos.astype(jnp.float32)[:, None] * inv_freq
        cos, sin = jnp.cos(ang), jnp.sin(ang)
        x1 = xp[..., :quarter].astype(jnp.float32)
        x2 = xp[..., quarter:].astype(jnp.float32)
        return jnp.concatenate([x1 * cos - x2 * sin, x2 * cos + x1 * sin], axis=-1)

    return jnp.concatenate([rotate(x[..., :half], rows), rotate(x[..., half:], cols)], axis=-1).astype(x.dtype)


def softmax_attend(q, k, v):
    s = jnp.einsum('bhqd,bhkd->bhqk', q, k).astype(jnp.float32) * (q.shape[-1] ** -0.5)
    p = jax.nn.softmax(s, axis=-1)
    return jnp.einsum('bhqk,bhkd->bhqd', p.astype(v.dtype), v)


def diff_attend(q, k, v, lam):
    s = jnp.einsum('bhmqd,bhmkd->bhmqk', q, k).astype(jnp.float32) * (q.shape[-1] ** -0.5)
    p = jax.nn.softmax(s, axis=-1)
    p = p[:, :, 0] - lam * p[:, :, 1]
    return jnp.einsum('bhqk,bhkd->bhqd', p.astype(v.dtype), v)


def neighbourhood_attention(q, k, v, kc, vc, rpb):
    b, h, t, hd = q.shape
    rows = t // GRID_W
    kh = min(NA_WIN_ROWS, rows)
    kw = NA_WIN_COLS
    scale = hd ** -0.5
    qg = q.reshape(b, h, rows, GRID_W, hd)
    kg = k.reshape(b, h, rows, GRID_W, hd)
    vg = v.reshape(b, h, rows, GRID_W, hd)
    col = jnp.arange(GRID_W)
    col_start = jnp.clip(col - kw // 2, 0, GRID_W - kw)
    col_ok = (col[None, :] >= col_start[:, None]) & (col[None, :] < col_start[:, None] + kw)
    col_idx = jnp.clip(col[None, :] - col[:, None] + kw - 1, 0, 2 * kw - 2)
    rpb_cols = rpb[:, :, col_idx]

    def one_row(r):
        r0 = jnp.clip(r - kh // 2, 0, rows - kh)
        qr = lax.dynamic_index_in_dim(qg, r, axis=2, keepdims=False)
        kr = lax.dynamic_slice_in_dim(kg, r0, kh, axis=2)
        vr = lax.dynamic_slice_in_dim(vg, r0, kh, axis=2)
        dr = r0 + jnp.arange(kh) - r + NA_WIN_ROWS - 1
        bias = jnp.take(rpb_cols, dr, axis=1).transpose(0, 2, 1, 3).astype(jnp.float32)
        s_win = jnp.einsum('bhqd,bhawd->bhqaw', qr, kr).astype(jnp.float32) * scale + bias[None]
        s_win = jnp.where(col_ok[:, None, :], s_win, NEG_INF).reshape(b, h, GRID_W, kh * GRID_W)
        s_ctx = jnp.einsum('bhqd,bhcd->bhqc', qr, kc).astype(jnp.float32) * scale
        p = jax.nn.softmax(jnp.concatenate([s_win, s_ctx], axis=-1), axis=-1).astype(v.dtype)
        p_win = p[..., :kh * GRID_W].reshape(b, h, GRID_W, kh, GRID_W)
        p_ctx = p[..., kh * GRID_W:]
        return jnp.einsum('bhqaw,bhawd->bhqd', p_win, vr) + jnp.einsum('bhqc,bhcd->bhqd', p_ctx, vc)

    out = lax.map(one_row, jnp.arange(rows))
    return out.transpose(1, 2, 0, 3, 4).reshape(b, h, t, hd)


def centred_dwconv(x, w):
    k, c = w.shape
    return lax.conv_general_dilated(x, w[:, None, :].astype(x.dtype), window_strides=(1,),
                                    padding=[(k // 2, k // 2)], dimension_numbers=('NWC', 'WIO', 'NWC'),
                                    feature_group_count=c)


def _chunks_first(t):
    return jnp.moveaxis(t, 2, 0)


def gla_chunk(q, k, v, g, s0, with_output):
    b, h, t, dk = k.shape
    dv = v.shape[-1]
    n = t // CHUNK
    q, k, v, g = [a.reshape(b, h, n, CHUNK, a.shape[-1]) for a in (q, k, v, g)]
    G = jnp.cumsum(g, axis=3)
    G_last = G[:, :, :, -1:, :]
    k_tail = k * jnp.exp(G_last - G)
    decay = jnp.exp(G_last[:, :, :, 0, :])
    xs = (_chunks_first(k_tail), _chunks_first(v), _chunks_first(decay))
    if with_output:
        causal = jnp.tril(jnp.ones((CHUNK, CHUNK), dtype=bool))
        a = jnp.einsum('bhncd,bhnsd->bhncs', q * jnp.exp(G - G_last), k_tail)
        o_intra = jnp.einsum('bhncs,bhnse->bhnce', jnp.where(causal, a, 0.0), v)
        xs = xs + (_chunks_first(q * jnp.exp(G)),)

    def step(S, xs_n):
        k_n, v_n, d_n = xs_n[:3]
        o = jnp.einsum('bhcd,bhde->bhce', xs_n[3], S) if with_output else None
        return S * d_n[..., None] + jnp.einsum('bhcd,bhce->bhde', k_n, v_n), o

    S, o_inter = lax.scan(step, s0, xs)
    if not with_output:
        return None, S
    return (jnp.moveaxis(o_inter, 0, 2) + o_intra).reshape(b, h, t, dv), S


def gdn_chunk(q, k, v, g, beta, s0, with_output):
    b, h, t, dk = k.shape
    dv = v.shape[-1]
    n = t // CHUNK
    q, k, v = [a.reshape(b, h, n, CHUNK, a.shape[-1]) for a in (q, k, v)]
    g, beta = [a.reshape(b, h, n, CHUNK) for a in (g, beta)]
    G = jnp.cumsum(g, axis=-1)
    idx = jnp.arange(CHUNK)
    lower = idx[:, None] >= idx[None, :]
    strict = idx[:, None] > idx[None, :]
    gamma = jnp.exp(jnp.where(lower, G[..., :, None] - G[..., None, :], -jnp.inf))
    kb = k * beta[..., None]
    a = jnp.where(strict, jnp.einsum('bhncd,bhnsd->bhncs', kb, k) * gamma, 0.0)
    w = lax.linalg.triangular_solve(a, kb * jnp.exp(G)[..., None], left_side=True, lower=True, unit_diagonal=True)
    u = lax.linalg.triangular_solve(a, v * beta[..., None], left_side=True, lower=True, unit_diagonal=True)
    k_tail = k * jnp.exp(G[..., -1:] - G)[..., None]
    decay = jnp.exp(G[..., -1])
    xs = (_chunks_first(w), _chunks_first(u), _chunks_first(k_tail), _chunks_first(decay))
    if with_output:
        a_qk = jnp.einsum('bhncd,bhnsd->bhncs', q, k) * gamma
        xs = xs + (_chunks_first(q * jnp.exp(G)[..., None]), _chunks_first(a_qk))

    def step(S, xs_n):
        w_n, u_n, kt_n, d_n = xs_n[:4]
        v_new = u_n - jnp.einsum('bhcd,bhde->bhce', w_n, S)
        o = None
        if with_output:
            o = jnp.einsum('bhcd,bhde->bhce', xs_n[4], S) + jnp.einsum('bhcs,bhse->bhce', xs_n[5], v_new)
        return S * d_n[..., None, None] + jnp.einsum('bhcd,bhce->bhde', kt_n, v_new), o

    S, o = lax.scan(step, s0, xs)
    if not with_output:
        return None, S
    return jnp.moveaxis(o, 0, 2).reshape(b, h, t, dv), S


def directional_scan(chunk_fn, ctx_in, lat_in, reverse, need_ctx):
    flip = (lambda a: jnp.flip(a, axis=2)) if reverse else (lambda a: a)
    ctx_in = tuple(flip(a) for a in ctx_in)
    lat_in = tuple(flip(a) for a in lat_in)
    b, h, _, dk = ctx_in[1].shape
    s0 = jnp.zeros((b, h, dk, ctx_in[2].shape[-1]), jnp.float32)
    o_ctx, s_ctx = chunk_fn(*ctx_in, s0, need_ctx)
    o_lat, _ = chunk_fn(*lat_in, s_ctx, True)
    return (flip(o_ctx) if need_ctx else None), flip(o_lat)


def gated_head_norm(o, gate, w):
    return from_heads(rmsnorm(o, w)) * jax.nn.silu(gate.astype(jnp.float32))


def attention_mixer(h_lat, h_ctx, w_in, rpb, lam, subln, layer, need_ctx):
    b, t, _ = h_lat.shape
    cuts = [NA_W, 2 * NA_W, 3 * NA_W, 3 * NA_W + DF_W, 3 * NA_W + 2 * DF_W]

    def project(h):
        na_q, na_k, na_v, df_q, df_k, df_v = jnp.split(h @ w_in, cuts, axis=-1)
        bh, th, _ = h.shape
        pair = lambda a: a.reshape(bh, th, DIFF_HEADS, 2, DIFF_HEAD_DIM).transpose(0, 2, 3, 1, 4)
        return (to_heads(na_q, NA_HEADS), to_heads(na_k, NA_HEADS), to_heads(na_v, NA_HEADS),
                pair(df_q), pair(df_k), to_heads(df_v, DIFF_HEADS))

    nq, nk, nv, dq, dk, dv = project(h_lat)
    cnq, cnk, cnv, cdq, cdk, cdv = project(h_ctx)
    pos = jnp.arange(t)
    rows, cols = pos // GRID_W, pos % GRID_W
    dq = axial_rope(dq, rows, cols)
    dk = axial_rope(dk, rows, cols)
    lam_init = 0.8 - 0.6 * math.exp(-0.3 * layer)
    lq1, lk1, lq2, lk2 = lam.astype(jnp.float32)
    lam_full = jnp.exp(jnp.sum(lq1 * lk1)) - jnp.exp(jnp.sum(lq2 * lk2)) + lam_init

    def diff_post(o):
        return from_heads(rmsnorm(o, subln) * (1 - lam_init))

    o_na = neighbourhood_attention(nq, nk, nv, cnk, cnv, rpb)
    k_all = jnp.concatenate([cdk, dk], axis=3)
    v_all = jnp.concatenate([cdv, dv], axis=2)
    nb = t // DIFF_BLOCK
    q_blocks = jnp.moveaxis(dq.reshape(b, DIFF_HEADS, 2, nb, DIFF_BLOCK, DIFF_HEAD_DIM), 3, 0)
    o_df = lax.map(lambda qb: diff_attend(qb, k_all, v_all, lam_full), q_blocks)
    o_df = jnp.moveaxis(o_df, 0, 2).reshape(b, DIFF_HEADS, t, 2 * DIFF_HEAD_DIM)
    y_lat = jnp.concatenate([from_heads(o_na), diff_post(o_df)], axis=-1).astype(h_lat.dtype)
    y_ctx = None
    if need_ctx:
        y_ctx = jnp.concatenate([from_heads(softmax_attend(cnq, cnk, cnv)),
                                 diff_post(diff_attend(cdq, cdk, cdv, lam_full))], axis=-1).astype(h_ctx.dtype)
    return y_lat, y_ctx


def recurrent_mixer(h_lat, h_ctx, w_in, lb, conv_w, a_log, dt_bias, c_norm, d_norm, need_ctx):
    f32 = jnp.float32
    cuts = [HG_W, 2 * HG_W, 3 * HG_W, 4 * HG_W, 5 * HG_W, 5 * HG_W + 3 * GD_W, 5 * HG_W + 3 * GD_W + 4 * GDN_HEADS]
    a_rate = jnp.exp(a_log.astype(f32))

    def tokenwise(h):
        hq, hff, hfb, hi, hgate, dqkv, dab, dgate = jnp.split(h @ w_in, cuts, axis=-1)

        def forget(f, lb_dir):
            lb_dir = lb_dir.astype(f32)
            g = jnp.logaddexp(jnp.log(lb_dir), jnp.log1p(-lb_dir) + jax.nn.log_sigmoid(f.astype(f32)))
            return to_heads(-jnp.expm1(g), HGRN_HEADS), to_heads(g, HGRN_HEADS)

        def decay(a, bb, d):
            g = -a_rate[d] * jax.nn.softplus(a + dt_bias[d].astype(f32))
            return jnp.swapaxes(g, 1, 2), jnp.swapaxes(jax.nn.sigmoid(bb), 1, 2)

        qkv = jax.nn.silu(centred_dwconv(dqkv, conv_w)).astype(f32)
        dq, dk, dv = jnp.split(qkv, 3, axis=-1)
        a_f, a_b, b_f, b_b = jnp.split(dab.astype(f32), 4, axis=-1)
        tw = {}
        tw['c_q'] = to_heads(jax.nn.silu(hq.astype(f32)), HGRN_HEADS) * HGRN_HEAD_DIM ** -0.5
        tw['c_v'] = to_heads(hi.astype(f32), HGRN_HEADS)
        tw['c_k_fwd'], tw['c_g_fwd'] = forget(hff, lb[0])
        tw['c_k_bwd'], tw['c_g_bwd'] = forget(hfb, lb[1])
        tw['d_q'] = l2norm(to_heads(dq, GDN_HEADS)) * GDN_HEAD_DIM ** -0.5
        tw['d_k'] = l2norm(to_heads(dk, GDN_HEADS))
        tw['d_v'] = to_heads(dv, GDN_HEADS)
        tw['d_g_fwd'], tw['d_b_fwd'] = decay(a_f, b_f, 0)
        tw['d_g_bwd'], tw['d_b_bwd'] = decay(a_b, b_b, 1)
        tw['c_gate'] = hgate
        tw['d_gate'] = dgate
        return tw

    tl = tokenwise(h_lat)
    tc = tokenwise(h_ctx)
    o_c, o_d = [], []
    for d in ('fwd', 'bwd'):
        rev = d == 'bwd'
        hg_c = (tc['c_q'], tc['c_k_' + d], tc['c_v'], tc['c_g_' + d])
        hg_l = (tl['c_q'], tl['c_k_' + d], tl['c_v'], tl['c_g_' + d])
        gd_c = (tc['d_q'], tc['d_k'], tc['d_v'], tc['d_g_' + d], tc['d_b_' + d])
        gd_l = (tl['d_q'], tl['d_k'], tl['d_v'], tl['d_g_' + d], tl['d_b_' + d])
        o_c.append(directional_scan(gla_chunk, hg_c, hg_l, rev, need_ctx))
        o_d.append(directional_scan(gdn_chunk, gd_c, gd_l, rev, need_ctx))

    def merge(tw, i):
        yc = gated_head_norm(o_c[0][i] + o_c[1][i], tw['c_gate'], c_norm)
        yd = gated_head_norm(o_d[0][i] + o_d[1][i], tw['d_gate'], d_norm)
        return jnp.concatenate([yc, yd], axis=-1)

    y_lat = merge(tl, 1).astype(h_lat.dtype)
    y_ctx = merge(tc, 0).astype(h_ctx.dtype) if need_ctx else None
    return y_lat, y_ctx


def setup_inputs(seed: int = 0) -> dict:
    key = jax.random.key(seed)
    ks = jax.random.split(key, 24)
    f32 = jnp.float32
    D = D_MODEL

    def nrm(k, shape, s):
        return jax.random.normal(k, shape, f32) * s

    def gain(k, shape):
        return 1.0 + 0.05 * jax.random.normal(k, shape, f32)

    dt = jnp.exp(jax.random.uniform(ks[20], (N_REC_LAYERS, 2, GDN_HEADS), f32, math.log(1e-3), math.log(1e-1)))
    return {
        'x': nrm(ks[0], (BATCH, SEQ, D), 1.0),
        'c': nrm(ks[1], (BATCH, D), 1.0),
        'ctx': nrm(ks[2], (BATCH, CTX_LEN, D), 1.0),
        'c_ctx': nrm(ks[3], (D,), 1.0),
        'ada_w': nrm(ks[4], (DEPTH, D, 6 * D), 0.5 * D ** -0.5),
        'ada_b': nrm(ks[5], (DEPTH, 6 * D), 0.01),
        'norm_mix': gain(ks[6], (DEPTH, D)),
        'norm_ffn': gain(ks[7], (DEPTH, D)),
        'w_mix_out': nrm(ks[8], (DEPTH, MIX_W, D), MIX_W ** -0.5),
        'ffn_gate': nrm(ks[9], (DEPTH, D, D_FF), D ** -0.5),
        'ffn_up': nrm(ks[10], (DEPTH, D, D_FF), D ** -0.5),
        'ffn_down': nrm(ks[11], (DEPTH, D_FF, D), D_FF ** -0.5),
        'att_w_in': nrm(ks[12], (N_ATT_LAYERS, D, ATT_IN), D ** -0.5),
        'att_rpb': nrm(ks[13], (N_ATT_LAYERS, NA_HEADS, 2 * NA_WIN_ROWS - 1, 2 * NA_WIN_COLS - 1), 0.02),
        'att_lambda': nrm(ks[14], (N_ATT_LAYERS, 4, DIFF_HEAD_DIM), 0.1),
        'att_subln': gain(ks[15], (N_ATT_LAYERS, 2 * DIFF_HEAD_DIM)),
        'rec_w_in': nrm(ks[16], (N_REC_LAYERS, D, REC_IN), D ** -0.5),
        'rec_lb_logits': nrm(ks[17], (DEPTH, 2, HG_W), 0.1),
        'rec_conv_w': nrm(ks[18], (N_REC_LAYERS, SHORT_CONV, 3 * GD_W), SHORT_CONV ** -0.5),
        'rec_a_log': jnp.log(jax.random.uniform(ks[19], (N_REC_LAYERS, 2, GDN_HEADS), f32, 1.0, 16.0)),
        'rec_dt_bias': dt + jnp.log(-jnp.expm1(-dt)),
        'rec_c_norm': gain(ks[21], (N_REC_LAYERS, HGRN_HEAD_DIM)),
        'rec_d_norm': gain(ks[22], (N_REC_LAYERS, GDN_HEAD_DIM)),
        'final_norm': gain(ks[23], (D,)),
    }


def reference(x, c, ctx, c_ctx, ada_w, ada_b, norm_mix, norm_ffn, w_mix_out, ffn_gate, ffn_up, ffn_down,
              att_w_in, att_rpb, att_lambda, att_subln, rec_w_in, rec_lb_logits, rec_conv_w, rec_a_log,
              rec_dt_bias, rec_c_norm, rec_d_norm, final_norm):
    lbs = jax.nn.softmax(rec_lb_logits.astype(jnp.float32), axis=0)
    lbs = jnp.cumsum(lbs, axis=0) - lbs[0]
    c_lat = c[:, None, :]
    c_pre = c_ctx[None, None, :]
    h_ctx = ctx
    for layer in range(DEPTH):
        last = layer == DEPTH - 1
        j = layer // 2
        mod_l = adaln(c_lat, ada_w[layer], ada_b[layer])
        mod_c = adaln(c_pre, ada_w[layer], ada_b[layer])
        hl = modulate(rmsnorm(x, norm_mix[layer]), mod_l[0], mod_l[1])
        hc = modulate(rmsnorm(h_ctx, norm_mix[layer]), mod_c[0], mod_c[1])
        if layer % 2 == 0:
            y_lat, y_ctx = attention_mixer(hl, hc, att_w_in[j], att_rpb[j], att_lambda[j], att_subln[j],
                                           layer, not last)
        else:
            y_lat, y_ctx = recurrent_mixer(hl, hc, rec_w_in[j], lbs[layer], rec_conv_w[j], rec_a_log[j],
                                           rec_dt_bias[j], rec_c_norm[j], rec_d_norm[j], not last)
        x = x + mod_l[2] * (y_lat @ w_mix_out[layer])
        x = x + mod_l[5] * swiglu(modulate(rmsnorm(x, norm_ffn[layer]), mod_l[3], mod_l[4]),
                                  ffn_gate[layer], ffn_up[layer], ffn_down[layer])
        if not last:
            h_ctx = h_ctx + mod_c[2] * (y_ctx @ w_mix_out[layer])
            h_ctx = h_ctx + mod_c[5] * swiglu(modulate(rmsnorm(h_ctx, norm_ffn[layer]), mod_c[3], mod_c[4]),
                                              ffn_gate[layer], ffn_up[layer], ffn_down[layer])
    return rmsnorm(x, final_norm)
```

```python
import functools
import math

import jax
import jax.numpy as jnp
from jax import lax
from jax.experimental import pallas as pl
from jax.experimental.pallas import tpu as pltpu

F32 = jnp.float32
BF16 = jnp.bfloat16

EPS = 1e-6
MASK_VALUE = -1e30
GRID_W = 64
NA_HEADS = 8
NA_HEAD_DIM = 64
NA_WIN_ROWS = 8
NA_WIN_COLS = 16
DIFF_HEADS = 4
DIFF_HEAD_DIM = 64
REC_HEADS = 4
REC_HEAD_DIM = 128
CHUNK = 64
ROPE_THETA = 10000.0
LANES = 128
REC_BLOCK = 4 * CHUNK
VMEM_LIMIT = 56 * 1024 * 1024

_DQ, _DK, _DV, _HQ, _HFF, _HFB, _HI, _HGATE, _DGATE, _DAB = 0, 4, 8, 12, 16, 20, 24, 28, 32, 36
REC_PROJ_W = 5120

_NT = (((1,), (1,)), ((), ()))
_TN = (((0,), (0,)), ((), ()))


def _params(*sem):
    return pltpu.CompilerParams(dimension_semantics=sem, vmem_limit_bytes=VMEM_LIMIT)


def _dot(a, b):
    return jnp.dot(a, b, preferred_element_type=F32)


def _dot_nt(a, b):
    return lax.dot_general(a, b, _NT, preferred_element_type=F32)


def _dot_tn(a, b):
    return lax.dot_general(a, b, _TN, preferred_element_type=F32)


def _mask01(m):
    return jnp.where(m, 1.0, 0.0).astype(BF16)


def _split3(x):
    hi = x.astype(BF16)
    r = x - hi.astype(F32)
    mid = r.astype(BF16)
    lo = (r - mid.astype(F32)).astype(BF16)
    return hi, mid, lo


def _dot_exact_lhs(m_bf16, x):
    hi, mid, lo = _split3(x)
    return _dot(m_bf16, hi) + _dot(m_bf16, mid) + _dot(m_bf16, lo)


def _sigmoid_pair(f):
    t = jnp.exp(-jnp.abs(f))
    r = 1.0 / (1.0 + t)
    pos = f >= 0
    return jnp.where(pos, r, t * r), jnp.where(pos, t * r, r)


def _silu(x):
    return x * _sigmoid_pair(x)[0]


def _rms_mod(x, nw, shift, scale):
    y = x * lax.rsqrt(jnp.mean(x * x, axis=-1, keepdims=True) + EPS)
    return (y * nw) * (1.0 + scale) + shift


def _adaln_kernel(c_ref, w_ref, b_ref, o_ref):
    a = _silu(c_ref[...])
    w = w_ref[0]
    a1, a2, _ = _split3(a)
    w1 = w.astype(BF16)
    w2 = (w - w1.astype(F32)).astype(BF16)
    o_ref[0] = _dot(a1, w1) + _dot(a1, w2) + _dot(a2, w1) + b_ref[0]


def _adaln(cvecs, ada_w, ada_b):
    depth, d, n = ada_w.shape
    rows = cvecs.shape[0]
    tn = 1536
    return pl.pallas_call(
        _adaln_kernel,
        grid=(depth, n // tn),
        in_specs=[
            pl.BlockSpec((rows, d), lambda l, j: (0, 0)),
            pl.BlockSpec((1, d, tn), lambda l, j: (l, 0, j)),
            pl.BlockSpec((1, 1, tn), lambda l, j: (l, 0, j)),
        ],
        out_specs=pl.BlockSpec((1, rows, tn), lambda l, j: (l, 0, j)),
        out_shape=jax.ShapeDtypeStruct((depth, rows, n), F32),
        compiler_params=_params("parallel", "parallel"),
        name="adaln",
    )(cvecs, ada_w, ada_b.reshape(depth, 1, n))


def _proj_kernel(x_ref, nw_ref, sh_ref, sc_ref, w_ref, *rest, rope_tiles):
    if rope_tiles:
        cos_ref, sa_ref, sb_ref, o_ref, h_ref = rest
    else:
        o_ref, h_ref = rest
    j = pl.program_id(2)

    @pl.when(j == 0)
    def _():
        h_ref[...] = _rms_mod(x_ref[0], nw_ref[...], sh_ref[0], sc_ref[0]).astype(BF16)

    acc = _dot(h_ref[...], w_ref[...])
    if not rope_tiles:
        o_ref[0] = acc.astype(o_ref.dtype)
        return
    is_rope = functools.reduce(jnp.logical_or, [j == t for t in rope_tiles])
    reps = acc.shape[1] // LANES

    @pl.when(is_rope)
    def _():
        tile = lambda r: jnp.concatenate([r[...]] * reps, axis=1)
        n = acc.shape[1]
        out = (acc * tile(cos_ref)
               + pltpu.roll(acc, 16, 1) * tile(sa_ref)
               + pltpu.roll(acc, n - 16, 1) * tile(sb_ref))
        o_ref[0] = out.astype(o_ref.dtype)

    @pl.when(jnp.logical_not(is_rope))
    def _():
        o_ref[0] = acc.astype(o_ref.dtype)


def _proj(x, nw, shift, scale, w, out_dtype, tm, tn, rope=None, rope_tiles=()):
    b, t, d = x.shape
    n = w.shape[1]
    per_batch = shift.shape[0] > 1
    mod_spec = pl.BlockSpec((1, 1, d), (lambda bi, i, j: (bi, 0, 0)) if per_batch else (lambda bi, i, j: (0, 0, 0)))
    in_specs = [
        pl.BlockSpec((1, tm, d), lambda bi, i, j: (bi, i, 0)),
        pl.BlockSpec((1, d), lambda bi, i, j: (0, 0)),
        mod_spec, mod_spec,
        pl.BlockSpec((d, tn), lambda bi, i, j: (0, j)),
    ]
    args = [x, nw.reshape(1, d), shift, scale, w]
    if rope_tiles:
        in_specs += [pl.BlockSpec((tm, LANES), lambda bi, i, j: (i, 0))] * 3
        args += list(rope)
    return pl.pallas_call(
        functools.partial(_proj_kernel, rope_tiles=tuple(rope_tiles)),
        grid=(b, t // tm, n // tn),
        in_specs=in_specs,
        out_specs=pl.BlockSpec((1, tm, tn), lambda bi, i, j: (bi, i, j)),
        out_shape=jax.ShapeDtypeStruct((b, t, n), out_dtype),
        scratch_shapes=[pltpu.VMEM((tm, d), BF16)],
        compiler_params=_params("parallel", "parallel", "arbitrary"),
        name="norm_proj",
    )(*args)


def _rope_tables(t):
    pos = jnp.arange(t)
    rows, cols = (pos // GRID_W).astype(F32), (pos % GRID_W).astype(F32)
    quarter = DIFF_HEAD_DIM // 4
    inv_freq = ROPE_THETA ** (-jnp.arange(quarter, dtype=F32) / quarter)
    lane = jnp.arange(LANES)
    within = lane % DIFF_HEAD_DIM
    use_row = within < DIFF_HEAD_DIM // 2
    freq = inv_freq[within % quarter]
    ang = jnp.where(use_row[None, :], rows[:, None], cols[:, None]) * freq[None, :]
    cos, sin = jnp.cos(ang), jnp.sin(ang)
    second = ((within // quarter) % 2) == 1
    sa = jnp.where(second[None, :], sin, 0.0)
    sb = jnp.where(second[None, :], 0.0, -sin)
    return cos, sa, sb


def _na_bias_table(rpb):
    kw = NA_WIN_COLS
    col = jnp.arange(GRID_W)
    col_start = jnp.clip(col - kw // 2, 0, GRID_W - kw)
    col_ok = (col[None, :] >= col_start[:, None]) & (col[None, :] < col_start[:, None] + kw)
    col_idx = jnp.clip(col[None, :] - col[:, None] + kw - 1, 0, 2 * kw - 2)
    rpb_cols = jnp.where(col_ok[None, None], rpb.astype(F32)[:, :, col_idx], MASK_VALUE)
    idx = jnp.arange(NA_WIN_ROWS)[:, None] + jnp.arange(NA_WIN_ROWS)[None, :]
    tb = rpb_cols[:, idx]
    tb = tb.transpose(0, 1, 3, 2, 4)
    return tb.reshape(rpb.shape[0], NA_WIN_ROWS, GRID_W, NA_WIN_ROWS * GRID_W)


def _softmax_pv(s_list, v_list):
    m = functools.reduce(jnp.maximum, [s.max(axis=-1, keepdims=True) for s in s_list])
    es = [jnp.exp(s - m) for s in s_list]
    l = functools.reduce(jnp.add, [e.sum(axis=-1, keepdims=True) for e in es])
    o = functools.reduce(jnp.add, [_dot(e.astype(BF16), v) for e, v in zip(es, v_list)])
    return o / l


def _na_kernel(q_ref, k_ref, v_ref, kc_ref, vc_ref, tb_ref, o_ref, *, rows, rows_per_step):
    i = pl.program_id(2)
    lo = lax.broadcasted_iota(jnp.int32, (GRID_W, LANES), 1) < NA_HEAD_DIM
    kc, vc = kc_ref[0], vc_ref[0]
    win = NA_WIN_ROWS * GRID_W

    def body(qa, carry):
        r = i * rows_per_step + qa
        r0 = jnp.clip(r - NA_WIN_ROWS // 2, 0, rows - NA_WIN_ROWS)
        off = r0 - r + NA_WIN_ROWS - 1
        q = q_ref[0, pl.ds(pl.multiple_of(qa * GRID_W, GRID_W), GRID_W), :] * (NA_HEAD_DIM ** -0.5)
        start = pl.multiple_of(r0 * GRID_W, GRID_W)
        kw = k_ref[0, pl.ds(start, win), :]
        vw = v_ref[0, pl.ds(start, win), :]
        outs = []
        for hh, sel in enumerate((lo, jnp.logical_not(lo))):
            qm = q * _mask01(sel)
            s_w = _dot_nt(qm, kw) + tb_ref[hh, off]
            s_c = _dot_nt(qm, kc)
            outs.append(_softmax_pv([s_w, s_c], [vw, vc]))
        o = jnp.where(lo, outs[0], outs[1])
        o_ref[0, pl.ds(pl.multiple_of(qa * GRID_W, GRID_W), GRID_W), :] = o.astype(o_ref.dtype)
        return carry

    lax.fori_loop(0, rows_per_step, body, 0)


def _na_attention(qkv, qkv_c, tb):
    b, t, _ = qkv.shape
    ctx = qkv_c.shape[1]
    rows = t // GRID_W
    rps = 8
    hp = NA_HEADS // 2
    tq = rps * GRID_W
    return pl.pallas_call(
        functools.partial(_na_kernel, rows=rows, rows_per_step=rps),
        grid=(b, hp, rows // rps),
        in_specs=[
            pl.BlockSpec((1, tq, LANES), lambda bi, h, i: (bi, i, h)),
            pl.BlockSpec((1, t, LANES), lambda bi, h, i: (bi, 0, hp + h)),
            pl.BlockSpec((1, t, LANES), lambda bi, h, i: (bi, 0, 2 * hp + h)),
            pl.BlockSpec((1, ctx, LANES), lambda bi, h, i: (bi, 0, hp + h)),
            pl.BlockSpec((1, ctx, LANES), lambda bi, h, i: (bi, 0, 2 * hp + h)),
            pl.BlockSpec((2, NA_WIN_ROWS, GRID_W, NA_WIN_ROWS * GRID_W), lambda bi, h, i: (h, 0, 0, 0)),
        ],
        out_specs=pl.BlockSpec((1, tq, LANES), lambda bi, h, i: (bi, i, h)),
        out_shape=jax.ShapeDtypeStruct((b, t, hp * LANES), BF16),
        compiler_params=_params("parallel", "parallel", "arbitrary"),
        name="na_attention",
    )(qkv, qkv, qkv, qkv_c, qkv_c, tb)


def _lam_full(lam_ref, lam_init):
    lv = lam_ref[...]
    s1 = jnp.sum(lv[0:1] * lv[1:2], axis=-1, keepdims=True)
    s2 = jnp.sum(lv[2:3] * lv[3:4], axis=-1, keepdims=True)
    return jnp.exp(s1) - jnp.exp(s2) + lam_init


def _diff_post(o, sub_ref, lam_init):
    y = o * lax.rsqrt(jnp.mean(o * o, axis=-1, keepdims=True) + EPS)
    return y * sub_ref[...] * (1.0 - lam_init)


def _diff_kernel(q_ref, kc_ref, vc_ref, *rest, tk, n_lat, lam_init):
    if n_lat:
        k_ref, v_ref, lam_ref, sub_ref, o_ref = rest
    else:
        lam_ref, sub_ref, o_ref = rest
    q = q_ref[0] * (DIFF_HEAD_DIM ** -0.5)
    tq = q.shape[0]
    lo = lax.broadcasted_iota(jnp.int32, (tq, LANES), 1) < DIFF_HEAD_DIM
    qs = (q * _mask01(lo), q * _mask01(jnp.logical_not(lo)))

    def chunk(kb, vb, st):
        new = []
        for qm, (m, l, acc) in zip(qs, st):
            s = _dot_nt(qm, kb)
            m_new = jnp.maximum(m, s.max(axis=-1, keepdims=True))
            alpha = jnp.exp(m - m_new)
            e = jnp.exp(s - m_new)
            l = alpha * l + e.sum(axis=-1, keepdims=True)
            acc = alpha * acc + _dot(e.astype(BF16), vb)
            new.append((m_new, l, acc))
        return tuple(new)

    init = (jnp.full((tq, 1), MASK_VALUE, F32), jnp.zeros((tq, 1), F32), jnp.zeros((tq, LANES), F32))
    st = chunk(kc_ref[0], vc_ref[0], (init, init))
    if n_lat:
        def body(c, st):
            start = pl.multiple_of(c * tk, tk)
            return chunk(k_ref[0, pl.ds(start, tk), :], v_ref[0, pl.ds(start, tk), :], st)
        st = lax.fori_loop(0, n_lat, body, st)
    (_, l1, a1), (_, l2, a2) = st
    o = a1 / l1 - _lam_full(lam_ref, lam_init) * (a2 / l2)
    o_ref[0] = _diff_post(o, sub_ref, lam_init).astype(o_ref.dtype)


def _diff_attention(q_src, qkv_c, qkv, lam, subln, lam_init, tq, tk):
    b, tqs, _ = q_src.shape
    ctx = qkv_c.shape[1]
    h0 = 3 * NA_HEADS // 2
    nh = DIFF_HEADS
    in_specs = [
        pl.BlockSpec((1, tq, LANES), lambda bi, h, i: (bi, i, h0 + h)),
        pl.BlockSpec((1, ctx, LANES), lambda bi, h, i: (bi, 0, h0 + nh + h)),
        pl.BlockSpec((1, ctx, LANES), lambda bi, h, i: (bi, 0, h0 + 2 * nh + h)),
    ]
    args = [q_src, qkv_c, qkv_c]
    n_lat = 0
    if qkv is not None:
        t = qkv.shape[1]
        n_lat = t // tk
        in_specs += [
            pl.BlockSpec((1, t, LANES), lambda bi, h, i: (bi, 0, h0 + nh + h)),
            pl.BlockSpec((1, t, LANES), lambda bi, h, i: (bi, 0, h0 + 2 * nh + h)),
        ]
        args += [qkv, qkv]
    in_specs += [
        pl.BlockSpec(lam.shape, lambda bi, h, i: (0, 0)),
        pl.BlockSpec((1, LANES), lambda bi, h, i: (0, 0)),
    ]
    args += [lam.astype(F32), subln.astype(F32).reshape(1, LANES)]
    return pl.pallas_call(
        functools.partial(_diff_kernel, tk=tk, n_lat=n_lat, lam_init=lam_init),
        grid=(b, nh, tqs // tq),
        in_specs=in_specs,
        out_specs=pl.BlockSpec((1, tq, LANES), lambda bi, h, i: (bi, i, h)),
        out_shape=jax.ShapeDtypeStruct((b, tqs, nh * LANES), BF16),
        compiler_params=_params("parallel", "parallel", "arbitrary"),
        name="diff_attention",
    )(*args)


def _ctx_na_kernel(q_ref, k_ref, v_ref, o_ref):
    q = q_ref[0] * (NA_HEAD_DIM ** -0.5)
    k, v = k_ref[0], v_ref[0]
    lo = lax.broadcasted_iota(jnp.int32, q.shape, 1) < NA_HEAD_DIM
    o0 = _softmax_pv([_dot_nt(q * _mask01(lo), k)], [v])
    o1 = _softmax_pv([_dot_nt(q * _mask01(jnp.logical_not(lo)), k)], [v])
    o_ref[0] = jnp.where(lo, o0, o1).astype(o_ref.dtype)


def _ctx_na_attention(qkv_c):
    b, ctx, _ = qkv_c.shape
    hp = NA_HEADS // 2
    return pl.pallas_call(
        _ctx_na_kernel,
        grid=(b, hp),
        in_specs=[
            pl.BlockSpec((1, ctx, LANES), lambda bi, h: (bi, 0, h)),
            pl.BlockSpec((1, ctx, LANES), lambda bi, h: (bi, 0, hp + h)),
            pl.BlockSpec((1, ctx, LANES), lambda bi, h: (bi, 0, 2 * hp + h)),
        ],
        out_specs=pl.BlockSpec((1, ctx, LANES), lambda bi, h: (bi, 0, h)),
        out_shape=jax.ShapeDtypeStruct((b, ctx, hp * LANES), BF16),
        compiler_params=_params("parallel", "parallel"),
        name="ctx_na_attention",
    )(qkv_c, qkv_c, qkv_c)


def _out_kernel(x_ref, ya_ref, yb_ref, wa_ref, wb_ref, g_ref, o_ref):
    y = _dot(ya_ref[0], wa_ref[...]) + _dot(yb_ref[0], wb_ref[...])
    o_ref[0] = x_ref[0] + g_ref[0] * y


def _mod_spec(arr, d, nidx):
    per_batch = arr.shape[0] > 1
    if nidx == 2:
        return pl.BlockSpec((1, 1, d), (lambda bi, i: (bi, 0, 0)) if per_batch else (lambda bi, i: (0, 0, 0)))
    return pl.BlockSpec((1, 1, d), (lambda bi, i, j: (bi, 0, 0)) if per_batch else (lambda bi, i, j: (0, 0, 0)))


def _out_proj(x, ya, yb, wa, wb, gate, tm):
    b, t, d = x.shape
    ka, kb = ya.shape[2], yb.shape[2]
    return pl.pallas_call(
        _out_kernel,
        grid=(b, t // tm),
        in_specs=[
            pl.BlockSpec((1, tm, d), lambda bi, i: (bi, i, 0)),
            pl.BlockSpec((1, tm, ka), lambda bi, i: (bi, i, 0)),
            pl.BlockSpec((1, tm, kb), lambda bi, i: (bi, i, 0)),
            pl.BlockSpec((ka, d), lambda bi, i: (0, 0)),
            pl.BlockSpec((kb, d), lambda bi, i: (0, 0)),
            _mod_spec(gate, d, 2),
        ],
        out_specs=pl.BlockSpec((1, tm, d), lambda bi, i: (bi, i, 0)),
        out_shape=jax.ShapeDtypeStruct((b, t, d), F32),
        compiler_params=_params("parallel", "parallel"),
        name="out_proj",
    )(x, ya, yb, wa, wb, gate)


def _ffn_kernel(x_ref, nw_ref, sh_ref, sc_ref, g_ref, wg_ref, wu_ref, wd_ref, *rest, final):
    if final:
        fn_ref, o_ref = rest
    else:
        (o_ref,) = rest
    x = x_ref[0]
    h = _rms_mod(x, nw_ref[...], sh_ref[0], sc_ref[0]).astype(BF16)
    a = _dot(h, wg_ref[...])
    u = _dot(h, wu_ref[...])
    z = (_silu(a) * u).astype(BF16)
    out = x + g_ref[0] * _dot(z, wd_ref[...])
    if final:
        out = out * lax.rsqrt(jnp.mean(out * out, axis=-1, keepdims=True) + EPS) * fn_ref[...]
    o_ref[0] = out


def _ffn(x, nw, shift, scale, gate, wg, wu, wd, final_norm, tm):
    b, t, d = x.shape
    f = wg.shape[1]
    const = lambda shape: pl.BlockSpec(shape, lambda bi, i: (0,) * len(shape), pipeline_mode=pl.Buffered(1))
    in_specs = [
        pl.BlockSpec((1, tm, d), lambda bi, i: (bi, i, 0)),
        pl.BlockSpec((1, d), lambda bi, i: (0, 0)),
        _mod_spec(shift, d, 2), _mod_spec(scale, d, 2), _mod_spec(gate, d, 2),
        const((d, f)), const((d, f)), const((f, d)),
    ]
    args = [x, nw.reshape(1, d), shift, scale, gate, wg, wu, wd]
    if final_norm is not None:
        in_specs.append(pl.BlockSpec((1, d), lambda bi, i: (0, 0)))
        args.append(final_norm.reshape(1, d))
    return pl.pallas_call(
        functools.partial(_ffn_kernel, final=final_norm is not None),
        grid=(b, t // tm),
        in_specs=in_specs,
        out_specs=pl.BlockSpec((1, tm, d), lambda bi, i: (bi, i, 0)),
        out_shape=jax.ShapeDtypeStruct((b, t, d), F32),
        compiler_params=_params("parallel", "parallel"),
        name="ffn",
    )(*args)


def _chunk_masks(n):
    row = lax.broadcasted_iota(jnp.int32, (n, n), 0)
    col = lax.broadcasted_iota(jnp.int32, (n, n), 1)
    shift = int(math.log2(CHUNK))
    same = lax.shift_right_logical(row, shift) == lax.shift_right_logical(col, shift)
    return row, col, same


def _rec_prep_kernel(dq_ref, dk_ref, dv_ref, pq_ref, pk_ref, pv_ref, nq_ref, nk_ref, nv_ref,
                     hq_ref, hff_ref, hfb_ref, hi_ref, dab_ref, cwq_ref, cwk_ref, cwv_ref, lb_ref, ab_ref,
                     cqe_ref, ckt_ref, coi_ref, cdec_ref, cv_ref,
                     dw_ref, du_ref, dkt_ref, dqe_ref, daqk_ref, ddec_ref, *, nblk):
    i = pl.program_id(1)
    h = pl.program_id(2)
    n = REC_BLOCK
    row, col, same = _chunk_masks(n)
    ones_bd = _mask01(same)
    incl = (same & (row >= col), same & (row <= col))
    strict = (same & (row > col), same & (row < col))
    cum = tuple(_mask01(m) for m in incl)
    r8 = lax.broadcasted_iota(jnp.int32, (8, n), 0)
    c8 = lax.broadcasted_iota(jnp.int32, (8, n), 1)
    sel = _mask01(lax.shift_right_logical(c8, int(math.log2(CHUNK))) == r8)
    rid = lax.broadcasted_iota(jnp.int32, (n, LANES), 0)

    cq = _silu(hq_ref[0]) * (REC_HEAD_DIM ** -0.5)
    cv = hi_ref[0]
    cvb = cv.astype(BF16)
    cv_ref[0] = cvb
    for d, f_ref in enumerate((hff_ref, hfb_ref)):
        lbv = lb_ref[0, pl.ds(d, 1), :]
        sig, nsig = _sigmoid_pair(f_ref[0])
        g = jnp.log(lbv + (1.0 - lbv) * sig)
        ck = (1.0 - lbv) * nsig
        gs = jnp.concatenate(_split3(g), axis=1)
        fold = lambda y: y[:, :LANES] + y[:, LANES:2 * LANES] + y[:, 2 * LANES:]
        G = fold(_dot(cum[d], gs))
        Gl = fold(_dot(ones_bd, gs))
        ktail = (ck * jnp.exp(Gl - G)).astype(BF16)
        qin = (cq * jnp.exp(G - Gl)).astype(BF16)
        a = jnp.where(incl[d], _dot_nt(qin, ktail), 0.0)
        coi_ref[d, 0] = _dot(a.astype(BF16), cvb)
        cqe_ref[d, 0] = (cq * jnp.exp(G)).astype(BF16)
        ckt_ref[d, 0] = ktail
        cdec_ref[d, 0, 0] = jnp.exp(fold(_dot(sel, gs)))

    def conv_silu(x_ref, p_ref, n_ref, cw_ref):
        x = x_ref[0]
        cw = cw_ref[...]
        prev_row = jnp.where(i > 0, p_ref[0, 7:8, :], 0.0)
        next_row = jnp.where(i < nblk - 1, n_ref[0, 0:1, :], 0.0)
        xp = jnp.where(rid == 0, prev_row, pltpu.roll(x, 1, 0))
        xn = jnp.where(rid == n - 1, next_row, pltpu.roll(x, n - 1, 0))
        return _silu(cw[0:1] * xp + cw[1:2] * x + cw[2:3] * xn)

    def l2n(x):
        return x * lax.rsqrt(jnp.sum(x * x, axis=-1, keepdims=True) + EPS)

    qn = l2n(conv_silu(dq_ref, pq_ref, nq_ref, cwq_ref)) * (REC_HEAD_DIM ** -0.5)
    kn = l2n(conv_silu(dk_ref, pk_ref, nk_ref, cwk_ref))
    vv = conv_silu(dv_ref, pv_ref, nv_ref, cwv_ref)
    knb = kn.astype(BF16)
    kk = _dot_nt(knb, knb)
    qk = _dot_nt(qn.astype(BF16), knb)
    tile = dab_ref[0]
    arate, dtb = ab_ref[0:1, :], ab_ref[1:2, :]
    xs = tile + dtb
    gt = -arate * (jnp.maximum(xs, 0.0) + jnp.log(1.0 + jnp.exp(-jnp.abs(xs))))
    bt = _sigmoid_pair(tile)[0]
    lane = lax.broadcasted_iota(jnp.int32, (n, LANES), 1)
    eye = (row == col).astype(F32)
    levels = int(math.log2(CHUNK))
    for d in range(2):
        pick = lambda t, j: jnp.sum(jnp.where(lane == j, t, 0.0), axis=-1, keepdims=True)
        gb = jnp.broadcast_to(pick(gt, d * REC_HEADS + h), (n, LANES))
        bb = jnp.broadcast_to(pick(bt, 2 * REC_HEADS + d * REC_HEADS + h), (n, LANES))
        gb2 = jnp.concatenate([gb, gb], axis=1)
        bb2 = jnp.concatenate([bb, bb], axis=1)
        Gc = _dot_exact_lhs(cum[d], gb)
        Gs = _dot_exact_lhs(ones_bd, jnp.where(incl[1 - d], gb2, 0.0))
        Gl = _dot_exact_lhs(ones_bd, gb)
        Gc2 = jnp.concatenate([Gc, Gc], axis=1)
        gamma = jnp.exp(jnp.where(incl[d], Gc2 - Gs, MASK_VALUE))
        a = jnp.where(strict[d], bb2 * kk * gamma, 0.0)
        T = eye
        for k in range(levels):
            blk = lax.shift_right_logical(row, k + 1) == lax.shift_right_logical(col, k + 1)
            sub = lax.shift_right_logical(row, k) != lax.shift_right_logical(col, k)
            ak = jnp.where(blk & sub, a, 0.0)
            if k == 0:
                T = T - ak
            else:
                Tb = T.astype(BF16)
                T = T - _dot(Tb, _dot(ak.astype(BF16), Tb).astype(BF16))
        eG = jnp.exp(Gc)
        rhs = jnp.concatenate([kn * bb * eG, vv * bb], axis=1).astype(BF16)
        wu = _dot(T.astype(BF16), rhs)
        dw_ref[d, 0] = wu[:, :LANES].astype(BF16)
        du_ref[d, 0] = wu[:, LANES:].astype(BF16)
        dkt_ref[d, 0] = (kn * jnp.exp(Gl - Gc)).astype(BF16)
        dqe_ref[d, 0] = (qn * eG).astype(BF16)
        aqk = jnp.where(incl[d], qk * gamma, 0.0)
        daqk_ref[d, 0] = (aqk[:, :LANES] + aqk[:, LANES:]).astype(BF16)
        ddec_ref[d, 0, 0] = jnp.exp(_dot_exact_lhs(sel, gb))


def _rec_prep(p, conv_w, lb, ab):
    b, t, _ = p.shape
    nblk = t // REC_BLOCK
    n = REC_BLOCK
    nh = REC_HEADS
    per8 = n // 8
    main = lambda base: pl.BlockSpec((1, n, LANES), lambda bi, i, h: (bi, i, base + h))
    prev = lambda base: pl.BlockSpec((1, 8, LANES), lambda bi, i, h: (bi, jnp.maximum(i * per8 - 1, 0), base + h))
    nxt = lambda base: pl.BlockSpec((1, 8, LANES),
                                    lambda bi, i, h: (bi, jnp.minimum((i + 1) * per8, t // 8 - 1), base + h))
    taps = lambda base: pl.BlockSpec((8, LANES), lambda bi, i, h: (0, base + h))
    in_specs = [main(_DQ), main(_DK), main(_DV), prev(_DQ), prev(_DK), prev(_DV), nxt(_DQ), nxt(_DK), nxt(_DV),
                main(_HQ), main(_HFF), main(_HFB), main(_HI),
                pl.BlockSpec((1, n, LANES), lambda bi, i, h: (bi, i, _DAB)),
                taps(_DQ), taps(_DK), taps(_DV),
                pl.BlockSpec((1, 8, LANES), lambda bi, i, h: (0, 0, h)),
                pl.BlockSpec((8, LANES), lambda bi, i, h: (0, 0))]
    dir_tok = lambda dt: (jax.ShapeDtypeStruct((2, b, t, nh * LANES), dt),
                          pl.BlockSpec((2, 1, n, LANES), lambda bi, i, h: (0, bi, i, h)))
    dir_dec = (jax.ShapeDtypeStruct((2, b, nblk, 8, nh * LANES), F32),
               pl.BlockSpec((2, 1, 1, 8, LANES), lambda bi, i, h: (0, bi, i, 0, h)))
    outs = [dir_tok(BF16), dir_tok(BF16), dir_tok(F32), dir_dec,
            (jax.ShapeDtypeStruct((b, t, nh * LANES), BF16), pl.BlockSpec((1, n, LANES), lambda bi, i, h: (bi, i, h))),
            dir_tok(BF16), dir_tok(BF16), dir_tok(BF16), dir_tok(BF16), dir_tok(BF16), dir_dec]
    return pl.pallas_call(
        functools.partial(_rec_prep_kernel, nblk=nblk),
        grid=(b, nblk, nh),
        in_specs=in_specs,
        out_specs=[o[1] for o in outs],
        out_shape=[o[0] for o in outs],
        compiler_params=_params("parallel", "parallel", "arbitrary"),
        name="rec_prep",
    )(*([p] * 14), conv_w, conv_w, conv_w, lb, ab)


def _rec_scan_kernel(cqe_ref, ckt_ref, coi_ref, cdec_ref, cv_ref, dw_ref, du_ref, dkt_ref, dqe_ref, daqk_ref,
                     ddec_ref, *rest, nblk, has_init, with_output):
    rest = list(rest)
    s0c_ref, s0d_ref = (rest.pop(0), rest.pop(0)) if has_init else (None, None)
    oc_ref, od_ref = (rest.pop(0), rest.pop(0)) if with_output else (None, None)
    sfc_ref, sfd_ref, sc_ref, sd_ref = rest
    d = pl.program_id(1)
    i = pl.program_id(2)

    @pl.when(i == 0)
    def _():
        if has_init:
            sc_ref[...] = s0c_ref[0, 0]
            sd_ref[...] = s0d_ref[0, 0]
        else:
            sc_ref[...] = jnp.zeros_like(sc_ref)
            sd_ref[...] = jnp.zeros_like(sd_ref)

    nchunk = REC_BLOCK // CHUNK
    for c in range(nchunk):
        cc = jnp.where(d == 0, c, nchunk - 1 - c)
        rows = pl.ds(pl.multiple_of(cc * CHUNK, CHUNK), CHUNK)
        is_row = lax.broadcasted_iota(jnp.int32, (8, LANES), 0) == cc
        pick_row = lambda tile: jnp.sum(jnp.where(is_row, tile, 0.0), axis=0, keepdims=True)
        for h in range(REC_HEADS):
            ln = slice(h * LANES, (h + 1) * LANES)
            st = sc_ref[h]
            if with_output:
                oc_ref[0, 0, rows, ln] = coi_ref[0, 0, rows, ln] + _dot_nt(cqe_ref[0, 0, rows, ln], st.astype(BF16))
            dec = pick_row(cdec_ref[0, 0, 0, :, ln])
            sc_ref[h] = st * dec + _dot_tn(cv_ref[0, rows, ln], ckt_ref[0, 0, rows, ln])
            s = sd_ref[h]
            sb = s.astype(BF16)
            vnew = du_ref[0, 0, rows, ln].astype(F32) - _dot(dw_ref[0, 0, rows, ln], sb)
            vnb = vnew.astype(BF16)
            if with_output:
                vpair = jnp.concatenate([vnb, vnb], axis=0)
                od_ref[0, 0, rows, ln] = _dot(dqe_ref[0, 0, rows, ln], sb) + _dot(daqk_ref[0, 0, rows, ln], vpair)
            ddec = pick_row(ddec_ref[0, 0, 0, :, ln])
            sd_ref[h] = s * ddec + _dot_tn(dkt_ref[0, 0, rows, ln], vnb)

    @pl.when(i == nblk - 1)
    def _():
        sfc_ref[0, 0] = sc_ref[...]
        sfd_ref[0, 0] = sd_ref[...]


def _rec_scan(prep, init, with_output):
    cqe, ckt, coi, cdec, cv, dw, du, dkt, dqe, daqk, ddec = prep
    _, b, t, w = cqe.shape
    nblk = t // REC_BLOCK
    n = REC_BLOCK
    blk = lambda d, i: jnp.where(d == 0, i, nblk - 1 - i)
    tok = pl.BlockSpec((1, 1, n, w), lambda bi, d, i: (d, bi, blk(d, i), 0))
    dec = pl.BlockSpec((1, 1, 1, 8, w), lambda bi, d, i: (d, bi, blk(d, i), 0, 0))
    state = pl.BlockSpec((1, 1, REC_HEADS, LANES, LANES), lambda bi, d, i: (bi, d, 0, 0, 0))
    in_specs = [tok, tok, tok, dec, pl.BlockSpec((1, n, w), lambda bi, d, i: (bi, blk(d, i), 0)),
                tok, tok, tok, tok, tok, dec]
    args = [cqe, ckt, coi, cdec, cv, dw, du, dkt, dqe, daqk, ddec]
    if init is not None:
        in_specs += [state, state]
        args += list(init)
    out_specs, out_shape = [], []
    if with_output:
        out_specs += [tok, tok]
        out_shape += [jax.ShapeDtypeStruct((2, b, t, w), F32)] * 2
    out_specs += [state, state]
    out_shape += [jax.ShapeDtypeStruct((b, 2, REC_HEADS, LANES, LANES), F32)] * 2
    res = pl.pallas_call(
        functools.partial(_rec_scan_kernel, nblk=nblk, has_init=init is not None, with_output=with_output),
        grid=(b, 2, nblk),
        in_specs=in_specs,
        out_specs=out_specs,
        out_shape=out_shape,
        scratch_shapes=[pltpu.VMEM((REC_HEADS, LANES, LANES), F32)] * 2,
        compiler_params=_params("parallel", "parallel", "arbitrary"),
        name="rec_scan",
    )(*args)
    return (res[0], res[1], (res[2], res[3])) if with_output else (None, None, (res[0], res[1]))


def _rec_out_kernel(x_ref, oc_ref, od_ref, cg_ref, dg_ref, cn_ref, dn_ref, wa_ref, wb_ref, g_ref, o_ref):
    def gated(dir_ref, gate_ref, nw_ref):
        o = dir_ref[0, 0] + dir_ref[1, 0]
        gate = _silu(gate_ref[0])
        parts = []
        for h in range(REC_HEADS):
            oh = o[:, h * LANES:(h + 1) * LANES]
            parts.append(oh * lax.rsqrt(jnp.mean(oh * oh, axis=-1, keepdims=True) + EPS) * nw_ref[...])
        return (jnp.concatenate(parts, axis=1) * gate).astype(BF16)

    y = _dot(gated(oc_ref, cg_ref, cn_ref), wa_ref[...]) + _dot(gated(od_ref, dg_ref, dn_ref), wb_ref[...])
    o_ref[0] = x_ref[0] + g_ref[0] * y


def _rec_out(x, oc, od, p, c_norm, d_norm, wa, wb, gate, tm):
    b, t, d = x.shape
    w = REC_HEADS * LANES
    dirs = pl.BlockSpec((2, 1, tm, w), lambda bi, i: (0, bi, i, 0))
    return pl.pallas_call(
        _rec_out_kernel,
        grid=(b, t // tm),
        in_specs=[
            pl.BlockSpec((1, tm, d), lambda bi, i: (bi, i, 0)),
            dirs, dirs,
            pl.BlockSpec((1, tm, w), lambda bi, i: (bi, i, _HGATE // REC_HEADS)),
            pl.BlockSpec((1, tm, w), lambda bi, i: (bi, i, _DGATE // REC_HEADS)),
            pl.BlockSpec((1, LANES), lambda bi, i: (0, 0)),
            pl.BlockSpec((1, LANES), lambda bi, i: (0, 0)),
            pl.BlockSpec((w, d), lambda bi, i: (0, 0)),
            pl.BlockSpec((w, d), lambda bi, i: (0, 0)),
            _mod_spec(gate, d, 2),
        ],
        out_specs=pl.BlockSpec((1, tm, d), lambda bi, i: (bi, i, 0)),
        out_shape=jax.ShapeDtypeStruct((b, t, d), F32),
        compiler_params=_params("parallel", "parallel"),
        name="rec_out",
    )(x, oc, od, p, p, c_norm.reshape(1, LANES), d_norm.reshape(1, LANES), wa, wb, gate)


def _pad_rows(a, rows):
    return jnp.pad(a, ((0, rows - a.shape[0]),) + ((0, 0),) * (a.ndim - 1))


def kernel(x, c, ctx, c_ctx, ada_w, ada_b, norm_mix, norm_ffn, w_mix_out, ffn_gate, ffn_up, ffn_down,
           att_w_in, att_rpb, att_lambda, att_subln, rec_w_in, rec_lb_logits, rec_conv_w, rec_a_log,
           rec_dt_bias, rec_c_norm, rec_d_norm, final_norm):
    b, t, d = x.shape
    depth = ada_w.shape[0]
    assert depth == 2 and t % (8 * GRID_W) == 0 and ctx.shape[1] % REC_BLOCK == 0
    tm = 512 if t % 512 == 0 else 256
    tmc = 256

    nrow = -(-(b + 1) // 8) * 8
    mods = _adaln(_pad_rows(jnp.concatenate([c, c_ctx[None, :]], axis=0).astype(F32), nrow), ada_w, ada_b)

    def layer_mods(layer):
        m = mods[layer].reshape(nrow, 6, 1, d)
        return [m[:b, k] for k in range(6)], [m[b:b + 1, k] for k in range(6)]

    bf = lambda a: a.astype(BF16)
    h_ctx = ctx

    ml, mc = layer_mods(0)
    w_in = bf(att_w_in[0])
    rope = _rope_tables(t)
    qkv = _proj(x, norm_mix[0], ml[0], ml[1], w_in, BF16, tm, 512, rope=rope, rope_tiles=(3, 4))
    qkv_c = _proj(h_ctx, norm_mix[0], mc[0], mc[1], w_in, BF16, tmc, 512)
    lam_init = 0.8 - 0.6 * math.exp(-0.3 * 0)
    tb = _na_bias_table(att_rpb[0])
    y_na = _na_attention(qkv, qkv_c, tb)
    y_df = _diff_attention(qkv, qkv_c, qkv, att_lambda[0], att_subln[0], lam_init, tq=256, tk=min(1024, t))
    yc_na = _ctx_na_attention(qkv_c)
    yc_df = _diff_attention(qkv_c, qkv_c, None, att_lambda[0], att_subln[0], lam_init, tq=qkv_c.shape[1], tk=0)
    half = NA_HEADS * NA_HEAD_DIM
    wo = bf(w_mix_out[0])
    wg, wu, wd = bf(ffn_gate[0]), bf(ffn_up[0]), bf(ffn_down[0])
    x = _out_proj(x, y_na, y_df, wo[:half], wo[half:], ml[2], tm)
    x = _ffn(x, norm_ffn[0], ml[3], ml[4], ml[5], wg, wu, wd, None, tm)
    h_ctx = _out_proj(h_ctx, yc_na, yc_df, wo[:half], wo[half:], mc[2], tmc)
    h_ctx = _ffn(h_ctx, norm_ffn[0], mc[3], mc[4], mc[5], wg, wu, wd, None, tmc)

    ml, mc = layer_mods(1)
    hw = REC_HEADS * REC_HEAD_DIM
    wr = rec_w_in[0]
    cuts = [0, hw, 2 * hw, 3 * hw, 4 * hw, 5 * hw, 8 * hw, 8 * hw + 4 * REC_HEADS, 9 * hw + 4 * REC_HEADS]
    hq, hff, hfb, hi, hgate, dqkv, dab, dgate = [wr[:, cuts[k]:cuts[k + 1]] for k in range(8)]
    dab = jnp.pad(dab, ((0, 0), (0, REC_PROJ_W - 9 * hw - dab.shape[1])))
    w_rec = bf(jnp.concatenate([dqkv, hq, hff, hfb, hi, hgate, dgate, dab], axis=1))
    lbs = jax.nn.softmax(rec_lb_logits.astype(F32), axis=0)
    lb = (jnp.cumsum(lbs, axis=0) - lbs[0])[1]
    lb = _pad_rows(lb, 8)[None]
    conv_w = _pad_rows(rec_conv_w[0].astype(F32), 8)
    a_rate = jnp.exp(rec_a_log[0].astype(F32)).reshape(-1)
    dtb = rec_dt_bias[0].astype(F32).reshape(-1)
    ab = jnp.zeros((8, LANES), F32).at[0, :a_rate.shape[0]].set(a_rate).at[1, :dtb.shape[0]].set(dtb)

    p_c = _proj(h_ctx, norm_mix[1], mc[0], mc[1], w_rec, F32, tmc, 512)
    p_l = _proj(x, norm_mix[1], ml[0], ml[1], w_rec, F32, tm, 512)
    _, _, s_ctx = _rec_scan(_rec_prep(p_c, conv_w, lb, ab), None, with_output=False)
    oc, od, _ = _rec_scan(_rec_prep(p_l, conv_w, lb, ab), s_ctx, with_output=True)
    wo = bf(w_mix_out[1])
    x = _rec_out(x, oc, od, p_l, rec_c_norm[0].astype(F32), rec_d_norm[0].astype(F32), wo[:hw], wo[hw:], ml[2], tm)
    return _ffn(x, norm_ffn[1], ml[3], ml[4], ml[5], bf(ffn_gate[1]), bf(ffn_up[1]), bf(ffn_down[1]), final_norm, tm)
```

```python
import functools
import math

import jax
import jax.numpy as jnp
from jax import lax
from jax.experimental import pallas as pl
from jax.experimental.pallas import tpu as pltpu

F32 = jnp.float32
BF16 = jnp.bfloat16

EPS = 1e-6
MASK_VALUE = -1e30
GRID_W = 64
NA_HEADS = 8
NA_HEAD_DIM = 64
NA_WIN_ROWS = 8
NA_WIN_COLS = 16
DIFF_HEADS = 4
DIFF_HEAD_DIM = 64
REC_HEADS = 4
REC_HEAD_DIM = 128
CHUNK = 64
ROPE_THETA = 10000.0
LOG2E = math.log2(math.e)
LANES = 128
BF16_ROWS = 16
PAIR = 2 * CHUNK
REC_BLOCK = 4 * CHUNK
REC_W = REC_HEADS * REC_HEAD_DIM
VMEM_LIMIT = 56 * 1024 * 1024

P16_DQ, P16_DK, P16_DV, P16_HQ, P16_HI, P16_HGATE, P16_DGATE = range(7)
P32_HFF, P32_HFB = 0, 1
P32_DAB_TILE = 2 * REC_HEADS

_NT = (((1,), (1,)), ((), ()))
_TN = (((0,), (0,)), ((), ()))


def _params(*sem):
    return pltpu.CompilerParams(dimension_semantics=sem, vmem_limit_bytes=VMEM_LIMIT)


def _resident(shape):
    return pl.BlockSpec(shape, lambda *_: (0,) * len(shape), pipeline_mode=pl.Buffered(1))


def _dot(a, b):
    return jnp.dot(a, b, preferred_element_type=F32)


def _dot_nt(a, b):
    return lax.dot_general(a, b, _NT, preferred_element_type=F32)


def _dot_tn(a, b):
    return lax.dot_general(a, b, _TN, preferred_element_type=F32)


def _mask01(m):
    return jnp.where(m, 1.0, 0.0).astype(BF16)


def _sigmoid_pair(f):
    t = jnp.exp(-jnp.abs(f))
    r = 1.0 / (1.0 + t)
    pos = f >= 0
    return jnp.where(pos, r, t * r), jnp.where(pos, t * r, r)


def _silu(x):
    return x * _sigmoid_pair(x)[0]


def _rms_mod(x, nw, shift, scale):
    y = x * lax.rsqrt(jnp.mean(x * x, axis=-1, keepdims=True) + EPS)
    return (y * nw) * (1.0 + scale) + shift


def _mod_spec(arr, d):
    per_batch = arr.shape[0] > 1
    return pl.BlockSpec((1, 1, d), (lambda bi, i: (bi, 0, 0)) if per_batch else (lambda bi, i: (0, 0, 0)))


def _adaln_kernel(c_ref, w_ref, b_ref, o_ref):
    a = _silu(c_ref[...])
    w = w_ref[0]
    a1 = a.astype(BF16)
    a2 = (a - a1.astype(F32)).astype(BF16)
    w1 = w.astype(BF16)
    w2 = (w - w1.astype(F32)).astype(BF16)
    o_ref[0] = _dot(a1, w1) + _dot(a1, w2) + _dot(a2, w1) + b_ref[0]


def _adaln(cvecs, ada_w, ada_b):
    depth, d, n = ada_w.shape
    rows = cvecs.shape[0]
    tn = 1536
    return pl.pallas_call(
        _adaln_kernel,
        grid=(depth, n // tn),
        in_specs=[
            pl.BlockSpec((rows, d), lambda l, j: (0, 0)),
            pl.BlockSpec((1, d, tn), lambda l, j: (l, 0, j)),
            pl.BlockSpec((1, 1, tn), lambda l, j: (l, 0, j)),
        ],
        out_specs=pl.BlockSpec((1, rows, tn), lambda l, j: (l, 0, j)),
        out_shape=jax.ShapeDtypeStruct((depth, rows, n), F32),
        compiler_params=_params("parallel", "parallel"),
        name="adaln",
    )(cvecs, ada_w, ada_b.reshape(depth, 1, n))


def _proj_kernel(x_ref, nw_ref, sh_ref, sc_ref, w_ref, *rest, tn, rope_tiles):
    if rope_tiles:
        cos_ref, sa_ref, sb_ref, o_ref = rest
    else:
        (o_ref,) = rest
    h = _rms_mod(x_ref[0], nw_ref[...], sh_ref[0], sc_ref[0]).astype(BF16)
    n = w_ref.shape[1]
    for j in range(n // tn):
        cols = slice(j * tn, (j + 1) * tn)
        acc = _dot(h, w_ref[:, cols])
        if j in rope_tiles:
            tile = lambda r: jnp.concatenate([r[...]] * (tn // LANES), axis=1)
            acc = (acc * tile(cos_ref)
                   + pltpu.roll(acc, 16, 1) * tile(sa_ref)
                   + pltpu.roll(acc, tn - 16, 1) * tile(sb_ref))
        o_ref[0, :, cols] = acc.astype(o_ref.dtype)


def _proj(x, nw, shift, scale, w, out_dtype, tm, tn, rope=None, rope_tiles=()):
    b, t, d = x.shape
    n = w.shape[1]
    in_specs = [
        pl.BlockSpec((1, tm, d), lambda bi, i: (bi, i, 0)),
        pl.BlockSpec((1, d), lambda bi, i: (0, 0)),
        _mod_spec(shift, d), _mod_spec(scale, d),
        _resident((d, n)),
    ]
    args = [x, nw.reshape(1, d), shift, scale, w]
    if rope_tiles:
        in_specs += [pl.BlockSpec((tm, LANES), lambda bi, i: (i, 0))] * 3
        args += list(rope)
    return pl.pallas_call(
        functools.partial(_proj_kernel, tn=tn, rope_tiles=tuple(rope_tiles)),
        grid=(b, t // tm),
        in_specs=in_specs,
        out_specs=pl.BlockSpec((1, tm, n), lambda bi, i: (bi, i, 0)),
        out_shape=jax.ShapeDtypeStruct((b, t, n), out_dtype),
        compiler_params=_params("parallel", "parallel"),
        name="norm_proj",
    )(*args)


def _rope_tables(t):
    pos = jnp.arange(t)
    rows, cols = (pos // GRID_W).astype(F32), (pos % GRID_W).astype(F32)
    quarter = DIFF_HEAD_DIM // 4
    inv_freq = ROPE_THETA ** (-jnp.arange(quarter, dtype=F32) / quarter)
    lane = jnp.arange(LANES)
    within = lane % DIFF_HEAD_DIM
    use_row = within < DIFF_HEAD_DIM // 2
    freq = inv_freq[within % quarter]
    ang = jnp.where(use_row[None, :], rows[:, None], cols[:, None]) * freq[None, :]
    cos, sin = jnp.cos(ang), jnp.sin(ang)
    second = ((within // quarter) % 2) == 1
    sa = jnp.where(second[None, :], sin, 0.0)
    sb = jnp.where(second[None, :], 0.0, -sin)
    return cos, sa, sb


def _na_bias_table(rpb):
    kw = NA_WIN_COLS
    col = jnp.arange(GRID_W)
    col_start = jnp.clip(col - kw // 2, 0, GRID_W - kw)
    col_ok = (col[None, :] >= col_start[:, None]) & (col[None, :] < col_start[:, None] + kw)
    col_idx = jnp.clip(col[None, :] - col[:, None] + kw - 1, 0, 2 * kw - 2)
    rpb_cols = jnp.where(col_ok[None, None], rpb.astype(F32)[:, :, col_idx], MASK_VALUE)
    idx = jnp.arange(NA_WIN_ROWS)[:, None] + jnp.arange(NA_WIN_ROWS)[None, :]
    tb = rpb_cols[:, idx] * LOG2E
    nh = rpb.shape[0]
    tb = tb.reshape(nh // 2, 2, NA_WIN_ROWS, NA_WIN_ROWS, GRID_W, GRID_W).transpose(0, 2, 1, 4, 3, 5)
    return tb.reshape(nh // 2, NA_WIN_ROWS, 2 * GRID_W, NA_WIN_ROWS * GRID_W)


def _softmax_pv(s_list, v_list):
    m = functools.reduce(jnp.maximum, [s.max(axis=-1, keepdims=True) for s in s_list])
    es = [jnp.exp2(s - m) for s in s_list]
    l = functools.reduce(jnp.add, [e.sum(axis=-1, keepdims=True) for e in es])
    o = functools.reduce(jnp.add, [_dot(e.astype(BF16), v) for e, v in zip(es, v_list)])
    return o / l


def _stack_heads(q, lo):
    return jnp.concatenate([q * _mask01(lo), q * _mask01(jnp.logical_not(lo))], axis=0)


def _na_kernel(q_ref, k_ref, v_ref, kc_ref, vc_ref, tb_ref, o_ref, *, rows, rows_per_step):
    i = pl.program_id(2)
    lo = lax.broadcasted_iota(jnp.int32, (GRID_W, LANES), 1) < NA_HEAD_DIM
    kc, vc = kc_ref[0], vc_ref[0]
    win = NA_WIN_ROWS * GRID_W

    starts, scores = [], []
    for qa in range(rows_per_step):
        r = i * rows_per_step + qa
        r0 = jnp.clip(r - NA_WIN_ROWS // 2, 0, rows - NA_WIN_ROWS)
        off = r0 - r + NA_WIN_ROWS - 1
        qs = _stack_heads(q_ref[0, qa * GRID_W:(qa + 1) * GRID_W, :], lo)
        start = pl.multiple_of(r0 * GRID_W, GRID_W)
        starts.append(start)
        scores.append([_dot_nt(qs, k_ref[0, pl.ds(start, win), :]) + tb_ref[0, off], _dot_nt(qs, kc)])
    for qa in range(rows_per_step):
        o = _softmax_pv(scores[qa], [v_ref[0, pl.ds(starts[qa], win), :], vc])
        o = jnp.where(lo, o[:GRID_W], o[GRID_W:])
        o_ref[0, qa * GRID_W:(qa + 1) * GRID_W, :] = o.astype(o_ref.dtype)


def _na_attention(qkv, qkv_c, tb):
    b, t, _ = qkv.shape
    ctx = qkv_c.shape[1]
    rows = t // GRID_W
    rps = 8
    hp = NA_HEADS // 2
    tq = rps * GRID_W
    return pl.pallas_call(
        functools.partial(_na_kernel, rows=rows, rows_per_step=rps),
        grid=(b, hp, rows // rps),
        in_specs=[
            pl.BlockSpec((1, tq, LANES), lambda bi, h, i: (bi, i, h)),
            pl.BlockSpec((1, t, LANES), lambda bi, h, i: (bi, 0, hp + h)),
            pl.BlockSpec((1, t, LANES), lambda bi, h, i: (bi, 0, 2 * hp + h)),
            pl.BlockSpec((1, ctx, LANES), lambda bi, h, i: (bi, 0, hp + h)),
            pl.BlockSpec((1, ctx, LANES), lambda bi, h, i: (bi, 0, 2 * hp + h)),
            pl.BlockSpec((1, NA_WIN_ROWS, 2 * GRID_W, NA_WIN_ROWS * GRID_W), lambda bi, h, i: (h, 0, 0, 0)),
        ],
        out_specs=pl.BlockSpec((1, tq, LANES), lambda bi, h, i: (bi, i, h)),
        out_shape=jax.ShapeDtypeStruct((b, t, hp * LANES), BF16),
        compiler_params=_params("parallel", "parallel", "arbitrary"),
        name="na_attention",
    )(qkv, qkv, qkv, qkv_c, qkv_c, tb)


def _ctx_na_kernel(q_ref, k_ref, v_ref, o_ref):
    n = q_ref.shape[1]
    lo = lax.broadcasted_iota(jnp.int32, (n, LANES), 1) < NA_HEAD_DIM
    o = _softmax_pv([_dot_nt(_stack_heads(q_ref[0], lo), k_ref[0])], [v_ref[0]])
    o_ref[0] = jnp.where(lo, o[:n], o[n:]).astype(o_ref.dtype)


def _ctx_na_attention(qkv_c):
    b, ctx, _ = qkv_c.shape
    hp = NA_HEADS // 2
    return pl.pallas_call(
        _ctx_na_kernel,
        grid=(b, hp),
        in_specs=[
            pl.BlockSpec((1, ctx, LANES), lambda bi, h: (bi, 0, h)),
            pl.BlockSpec((1, ctx, LANES), lambda bi, h: (bi, 0, hp + h)),
            pl.BlockSpec((1, ctx, LANES), lambda bi, h: (bi, 0, 2 * hp + h)),
        ],
        out_specs=pl.BlockSpec((1, ctx, LANES), lambda bi, h: (bi, 0, h)),
        out_shape=jax.ShapeDtypeStruct((b, ctx, hp * LANES), BF16),
        compiler_params=_params("parallel", "parallel"),
        name="ctx_na_attention",
    )(qkv_c, qkv_c, qkv_c)


def _lam_full(lam_ref, lam_init):
    lv = lam_ref[...]
    s1 = jnp.sum(lv[0:1] * lv[1:2], axis=-1, keepdims=True)
    s2 = jnp.sum(lv[2:3] * lv[3:4], axis=-1, keepdims=True)
    return jnp.exp(s1) - jnp.exp(s2) + lam_init


def _diff_post(o, sub_ref, lam_init):
    y = o * lax.rsqrt(jnp.mean(o * o, axis=-1, keepdims=True) + EPS)
    return y * sub_ref[...] * (1.0 - lam_init)


def _diff_kernel(q_ref, kc_ref, vc_ref, *rest, tk, n_lat, lam_init, nstream):
    if n_lat:
        k_ref, v_ref, lam_ref, sub_ref, o_ref, vx_ref, vcx_ref = rest
    else:
        lam_ref, sub_ref, o_ref, vcx_ref = rest
    tq = q_ref.shape[1] // nstream
    lo = lax.broadcasted_iota(jnp.int32, (tq, LANES), 1) < DIFF_HEAD_DIM
    qs = [_stack_heads(q_ref[0, s * tq:(s + 1) * tq, :], lo) for s in range(nstream)]

    @pl.when(pl.program_id(2) == 0)
    def _():
        vcx_ref[:, :LANES] = vc_ref[0]
        vcx_ref[:, LANES:] = jnp.ones((vcx_ref.shape[0], LANES), BF16)
        if n_lat:
            vx_ref[:, :LANES] = v_ref[0]
            vx_ref[:, LANES:] = jnp.ones((vx_ref.shape[0], LANES), BF16)

    def chunk(kb, vxb, sts):
        scores = [_dot_nt(q, kb) for q in qs]
        new = []
        for s, (m, acc) in zip(scores, sts):
            m_new = jnp.maximum(m, s.max(axis=-1, keepdims=True))
            e = jnp.exp2((s - m_new).astype(BF16))
            new.append((m_new, jnp.exp2(m - m_new) * acc + _dot(e, vxb)))
        return tuple(new)

    st = (jnp.full((2 * tq, 1), MASK_VALUE, F32), jnp.zeros((2 * tq, 2 * LANES), F32))
    sts = chunk(kc_ref[0], vcx_ref[...], (st,) * nstream)
    if n_lat:
        def body(c, sts):
            start = pl.multiple_of(c * tk, tk)
            return chunk(k_ref[0, pl.ds(start, tk), :], vx_ref[pl.ds(start, tk), :], sts)
        sts = lax.fori_loop(0, n_lat, body, sts)
    lam_full = _lam_full(lam_ref, lam_init)
    for s, (_, acc) in enumerate(sts):
        o = acc[:, :LANES] / acc[:, LANES:LANES + 1]
        o = o[:tq] - lam_full * o[tq:]
        o_ref[0, s * tq:(s + 1) * tq, :] = _diff_post(o, sub_ref, lam_init).astype(o_ref.dtype)


def _diff_attention(q_src, qkv_c, qkv, lam, subln, lam_init, tq, tk, nstream=1):
    b, tqs, _ = q_src.shape
    ctx = qkv_c.shape[1]
    h0 = 3 * NA_HEADS // 2
    nh = DIFF_HEADS
    in_specs = [
        pl.BlockSpec((1, tq, LANES), lambda bi, h, i: (bi, i, h0 + h)),
        pl.BlockSpec((1, ctx, LANES), lambda bi, h, i: (bi, 0, h0 + nh + h)),
        pl.BlockSpec((1, ctx, LANES), lambda bi, h, i: (bi, 0, h0 + 2 * nh + h)),
    ]
    args = [q_src, qkv_c, qkv_c]
    n_lat = 0
    scratch = []
    if qkv is not None:
        t = qkv.shape[1]
        n_lat = t // tk
        in_specs += [
            pl.BlockSpec((1, t, LANES), lambda bi, h, i: (bi, 0, h0 + nh + h)),
            pl.BlockSpec((1, t, LANES), lambda bi, h, i: (bi, 0, h0 + 2 * nh + h)),
        ]
        args += [qkv, qkv]
        scratch.append(pltpu.VMEM((t, 2 * LANES), BF16))
    scratch.append(pltpu.VMEM((ctx, 2 * LANES), BF16))
    in_specs += [
        pl.BlockSpec(lam.shape, lambda bi, h, i: (0, 0)),
        pl.BlockSpec((1, LANES), lambda bi, h, i: (0, 0)),
    ]
    args += [lam.astype(F32), subln.astype(F32).reshape(1, LANES)]
    return pl.pallas_call(
        functools.partial(_diff_kernel, tk=tk, n_lat=n_lat, lam_init=lam_init, nstream=nstream),
        grid=(b, nh, tqs // tq),
        in_specs=in_specs,
        out_specs=pl.BlockSpec((1, tq, LANES), lambda bi, h, i: (bi, i, h)),
        out_shape=jax.ShapeDtypeStruct((b, tqs, nh * LANES), BF16),
        scratch_shapes=scratch,
        compiler_params=_params("parallel", "parallel", "arbitrary"),
        name="diff_attention",
    )(*args)


def _out_kernel(x_ref, ya_ref, yb_ref, wa_ref, wb_ref, g_ref, o_ref):
    y = _dot(ya_ref[0], wa_ref[...]) + _dot(yb_ref[0], wb_ref[...])
    o_ref[0] = x_ref[0] + g_ref[0] * y


def _out_proj(x, ya, yb, wa, wb, gate, tm):
    b, t, d = x.shape
    ka, kb = ya.shape[2], yb.shape[2]
    return pl.pallas_call(
        _out_kernel,
        grid=(b, t // tm),
        in_specs=[
            pl.BlockSpec((1, tm, d), lambda bi, i: (bi, i, 0)),
            pl.BlockSpec((1, tm, ka), lambda bi, i: (bi, i, 0)),
            pl.BlockSpec((1, tm, kb), lambda bi, i: (bi, i, 0)),
            pl.BlockSpec((ka, d), lambda bi, i: (0, 0)),
            pl.BlockSpec((kb, d), lambda bi, i: (0, 0)),
            _mod_spec(gate, d),
        ],
        out_specs=pl.BlockSpec((1, tm, d), lambda bi, i: (bi, i, 0)),
        out_shape=jax.ShapeDtypeStruct((b, t, d), F32),
        compiler_params=_params("parallel", "parallel"),
        name="out_proj",
    )(x, ya, yb, wa, wb, gate)


def _ffn_kernel(x_ref, nw_ref, sh_ref, sc_ref, g_ref, wg_ref, wu_ref, wd_ref, *rest, final):
    if final:
        fn_ref, o_ref = rest
    else:
        (o_ref,) = rest
    x = x_ref[0]
    h = _rms_mod(x, nw_ref[...], sh_ref[0], sc_ref[0]).astype(BF16)
    a = _dot(h, wg_ref[...])
    u = _dot(h, wu_ref[...])
    z = (_silu(a) * u).astype(BF16)
    out = x + g_ref[0] * _dot(z, wd_ref[...])
    if final:
        out = out * lax.rsqrt(jnp.mean(out * out, axis=-1, keepdims=True) + EPS) * fn_ref[...]
    o_ref[0] = out


def _ffn(x, nw, shift, scale, gate, wg, wu, wd, final_norm, tm):
    b, t, d = x.shape
    f = wg.shape[1]
    in_specs = [
        pl.BlockSpec((1, tm, d), lambda bi, i: (bi, i, 0)),
        pl.BlockSpec((1, d), lambda bi, i: (0, 0)),
        _mod_spec(shift, d), _mod_spec(scale, d), _mod_spec(gate, d),
        _resident((d, f)), _resident((d, f)), _resident((f, d)),
    ]
    args = [x, nw.reshape(1, d), shift, scale, gate, wg, wu, wd]
    if final_norm is not None:
        in_specs.append(pl.BlockSpec((1, d), lambda bi, i: (0, 0)))
        args.append(final_norm.reshape(1, d))
    return pl.pallas_call(
        functools.partial(_ffn_kernel, final=final_norm is not None),
        grid=(b, t // tm),
        in_specs=in_specs,
        out_specs=pl.BlockSpec((1, tm, d), lambda bi, i: (bi, i, 0)),
        out_shape=jax.ShapeDtypeStruct((b, t, d), F32),
        compiler_params=_params("parallel", "parallel"),
        name="ffn",
    )(*args)


def _chunk_masks(n):
    row = lax.broadcasted_iota(jnp.int32, (n, n), 0)
    col = lax.broadcasted_iota(jnp.int32, (n, n), 1)
    shift = int(math.log2(CHUNK))
    same = lax.shift_right_logical(row, shift) == lax.shift_right_logical(col, shift)
    return row, col, same


def _chunk_scan(x, reverse, pos):
    n = x.shape[0]
    s = 1
    while s < CHUNK:
        if reverse:
            x = x + jnp.where(pos < CHUNK - s, pltpu.roll(x, n - s, 0), 0.0)
        else:
            x = x + jnp.where(pos >= s, pltpu.roll(x, s, 0), 0.0)
        s *= 2
    return x


def _chunk_totals(G, reverse):
    n, w = G.shape
    r8 = lax.broadcasted_iota(jnp.int32, (8, w), 0)
    parts, dec = [], jnp.zeros((8, w), F32)
    for c in range(n // CHUNK):
        r = c * CHUNK + (0 if reverse else CHUNK - 1)
        tot = G[r:r + 1, :]
        parts.append(jnp.broadcast_to(tot, (CHUNK, w)))
        dec = jnp.where(r8 == c, jnp.broadcast_to(tot, (8, w)), dec)
    return jnp.concatenate(parts, axis=0), dec


def _rec_prep_kernel(dq_ref, dk_ref, dv_ref, pq_ref, pk_ref, pv_ref, nq_ref, nk_ref, nv_ref,
                     hq_ref, hi_ref, hff_ref, hfb_ref, dab_ref, cwq_ref, cwk_ref, cwv_ref, lb_ref, ab_ref,
                     cqe_ref, ckt_ref, coi_ref, cdec_ref, cv_ref,
                     dw_ref, du_ref, dkt_ref, dqe_ref, daqk_ref, ddec_ref, *, nblk):
    i = pl.program_id(1)
    n = REC_BLOCK
    w = REC_W
    row, col, same = _chunk_masks(PAIR)
    incl = (same & (row >= col), same & (row <= col))
    strict = (same & (row > col), same & (row < col))
    rid = lax.broadcasted_iota(jnp.int32, (n, w), 0)
    pos = rid & (CHUNK - 1)
    pos1 = pos[:, :LANES]
    pairs = [slice(p * PAIR, (p + 1) * PAIR) for p in range(n // PAIR)]
    heads = [slice(h * LANES, (h + 1) * LANES) for h in range(REC_HEADS)]
    keys = [(h, d, p) for h in range(REC_HEADS) for d in range(2) for p in range(len(pairs))]
    hp_keys = [(h, p) for h in range(REC_HEADS) for p in range(len(pairs))]

    cq = _silu(hq_ref[0].astype(F32)) * (REC_HEAD_DIM ** -0.5)
    cvb = hi_ref[0]
    cv_ref[0] = cvb
    qin, ktail = {}, {}
    for d, f_ref in enumerate((hff_ref, hfb_ref)):
        lbv = lb_ref[pl.ds(d, 1), :]
        sig, nsig = _sigmoid_pair(f_ref[0])
        g = jnp.log(lbv + (1.0 - lbv) * sig)
        ck = (1.0 - lbv) * nsig
        G = _chunk_scan(g, d == 1, pos)
        Gl, dec = _chunk_totals(G, d == 1)
        ktail[d] = (ck * jnp.exp(Gl - G)).astype(BF16)
        qin[d] = (cq * jnp.exp(G - Gl)).astype(BF16)
        cqe_ref[d, 0] = (cq * jnp.exp(G)).astype(BF16)
        ckt_ref[d, 0] = ktail[d]
        cdec_ref[d, 0, 0] = jnp.exp(dec)
    amat = {(h, d, p): jnp.where(incl[d], _dot_nt(qin[d][pairs[p], heads[h]], ktail[d][pairs[p], heads[h]]), 0.0)
            for h, d, p in keys}
    for h, d, p in keys:
        coi_ref[d, 0, pairs[p], heads[h]] = _dot(amat[h, d, p].astype(BF16), cvb[pairs[p], heads[h]])

    last = BF16_ROWS - 1

    def conv_silu(x_ref, p_ref, n_ref, cw_ref):
        x = x_ref[0].astype(F32)
        cw = cw_ref[...]
        prev_row = jnp.where(i > 0, p_ref[0, last:last + 1, :].astype(F32), 0.0)
        next_row = jnp.where(i < nblk - 1, n_ref[0, 0:1, :].astype(F32), 0.0)
        xp = jnp.where(rid == 0, prev_row, pltpu.roll(x, 1, 0))
        xn = jnp.where(rid == n - 1, next_row, pltpu.roll(x, n - 1, 0))
        return _silu(cw[0:1] * xp + cw[1:2] * x + cw[2:3] * xn)

    def l2n(x):
        return jnp.concatenate(
            [x[:, hs] * lax.rsqrt(jnp.sum(x[:, hs] * x[:, hs], axis=-1, keepdims=True) + EPS) for hs in heads], axis=1)

    qn = l2n(conv_silu(dq_ref, pq_ref, nq_ref, cwq_ref)) * (REC_HEAD_DIM ** -0.5)
    kn = l2n(conv_silu(dk_ref, pk_ref, nk_ref, cwk_ref))
    vv = conv_silu(dv_ref, pv_ref, nv_ref, cwv_ref)
    knb, qnb = kn.astype(BF16), qn.astype(BF16)
    kk = {(h, p): _dot_nt(knb[pairs[p], heads[h]], knb[pairs[p], heads[h]]) for h, p in hp_keys}
    qk = {(h, p): _dot_nt(qnb[pairs[p], heads[h]], knb[pairs[p], heads[h]]) for h, p in hp_keys}
    tile = dab_ref[0]
    arate, dtb = ab_ref[0:1, :], ab_ref[1:2, :]
    xs = tile + dtb
    gt = -arate * (jnp.maximum(xs, 0.0) + jnp.log(1.0 + jnp.exp(-jnp.abs(xs))))
    bt = _sigmoid_pair(tile)[0]
    lane = lax.broadcasted_iota(jnp.int32, (n, LANES), 1)
    spread = lambda t, j0: jnp.concatenate(
        [jnp.broadcast_to(jnp.sum(jnp.where(lane == j0 + h, t, 0.0), axis=-1, keepdims=True), (n, LANES))
         for h in range(REC_HEADS)], axis=1)
    eye = (row == col).astype(F32)
    rhs, gmat = {}, {}
    for d in range(2):
        gsc = _chunk_scan(gt, d == 1, pos1)
        Gc = spread(gsc, d * REC_HEADS)
        bb = spread(bt, (2 + d) * REC_HEADS)
        Gl, dec = _chunk_totals(Gc, d == 1)
        eG = jnp.exp(Gc)
        rhs[d] = ((kn * bb * eG).astype(BF16), (vv * bb).astype(BF16))
        for h, p in hp_keys:
            gcp = Gc[pairs[p], heads[h]]
            gamma = jnp.exp(jnp.where(incl[d], gcp - gcp.T, MASK_VALUE))
            gmat[h, d, p] = jnp.where(strict[d], bb[pairs[p], heads[h]] * kk[h, p] * gamma, 0.0)
            daqk_ref[d, 0, pairs[p], heads[h]] = jnp.where(incl[d], qk[h, p] * gamma, 0.0).astype(BF16)
        dkt_ref[d, 0] = (kn * jnp.exp(Gl - Gc)).astype(BF16)
        dqe_ref[d, 0] = (qn * eG).astype(BF16)
        ddec_ref[d, 0, 0] = jnp.exp(dec)

    T = {key: eye for key in keys}
    for k in range(int(math.log2(CHUNK))):
        blk = lax.shift_right_logical(row, k + 1) == lax.shift_right_logical(col, k + 1)
        sub = lax.shift_right_logical(row, k) != lax.shift_right_logical(col, k)
        ak = {key: jnp.where(blk & sub, gmat[key], 0.0) for key in keys}
        if k == 0:
            T = {key: T[key] - ak[key] for key in keys}
            continue
        Tb = {key: T[key].astype(BF16) for key in keys}
        x = {key: _dot(ak[key].astype(BF16), Tb[key]).astype(BF16) for key in keys}
        T = {key: T[key] - _dot(Tb[key], x[key]) for key in keys}
    for h, d, p in keys:
        both = jnp.concatenate([rhs[d][0][pairs[p], heads[h]], rhs[d][1][pairs[p], heads[h]]], axis=1)
        wu = _dot(T[h, d, p].astype(BF16), both)
        dw_ref[d, 0, pairs[p], heads[h]] = wu[:, :LANES].astype(BF16)
        du_ref[d, 0, pairs[p], heads[h]] = wu[:, LANES:].astype(BF16)


def _rec_prep(p16, p32, conv_w, lb, ab):
    b, t, _ = p16.shape
    nblk = t // REC_BLOCK
    n = REC_BLOCK
    w = REC_W
    per = n // BF16_ROWS
    main = lambda blk: pl.BlockSpec((1, n, w), lambda bi, i: (bi, i, blk))
    prev = lambda blk: pl.BlockSpec((1, BF16_ROWS, w), lambda bi, i: (bi, jnp.maximum(i * per - 1, 0), blk))
    nxt = lambda blk: pl.BlockSpec((1, BF16_ROWS, w),
                                   lambda bi, i: (bi, jnp.minimum((i + 1) * per, t // BF16_ROWS - 1), blk))
    taps = lambda blk: pl.BlockSpec((8, w), lambda bi, i: (0, blk))
    in_specs = [main(P16_DQ), main(P16_DK), main(P16_DV), prev(P16_DQ), prev(P16_DK), prev(P16_DV),
                nxt(P16_DQ), nxt(P16_DK), nxt(P16_DV), main(P16_HQ), main(P16_HI),
                main(P32_HFF), main(P32_HFB),
                pl.BlockSpec((1, n, LANES), lambda bi, i: (bi, i, P32_DAB_TILE)),
                taps(0), taps(1), taps(2),
                pl.BlockSpec((8, w), lambda bi, i: (0, 0)),
                pl.BlockSpec((8, LANES), lambda bi, i: (0, 0))]
    dir_tok = lambda dt: (jax.ShapeDtypeStruct((2, b, t, w), dt),
                          pl.BlockSpec((2, 1, n, w), lambda bi, i: (0, bi, i, 0)))
    dir_dec = (jax.ShapeDtypeStruct((2, b, nblk, 8, w), F32),
               pl.BlockSpec((2, 1, 1, 8, w), lambda bi, i: (0, bi, i, 0, 0)))
    outs = [dir_tok(BF16), dir_tok(BF16), dir_tok(F32), dir_dec,
            (jax.ShapeDtypeStruct((b, t, w), BF16), pl.BlockSpec((1, n, w), lambda bi, i: (bi, i, 0))),
            dir_tok(BF16), dir_tok(BF16), dir_tok(BF16), dir_tok(BF16), dir_tok(BF16), dir_dec]
    return pl.pallas_call(
        functools.partial(_rec_prep_kernel, nblk=nblk),
        grid=(b, nblk),
        in_specs=in_specs,
        out_specs=[o[1] for o in outs],
        out_shape=[o[0] for o in outs],
        compiler_params=_params("parallel", "arbitrary"),
        name="rec_prep",
    )(*([p16] * 11), p32, p32, p32, conv_w, conv_w, conv_w, lb, ab)


_N_PREP = 11


def _rec_scan_kernel(*refs, nblk, has_init, with_output):
    ins = [refs[d * _N_PREP:(d + 1) * _N_PREP] for d in range(2)]
    rest = list(refs[2 * _N_PREP:])
    s0c_ref, s0d_ref = (rest.pop(0), rest.pop(0)) if has_init else (None, None)
    if with_output:
        oc_refs, od_refs = (rest.pop(0), rest.pop(0)), (rest.pop(0), rest.pop(0))
    sfc_ref, sfd_ref, sc_ref, sd_ref = rest
    i = pl.program_id(1)

    @pl.when(i == 0)
    def _():
        if has_init:
            sc_ref[...] = s0c_ref[0]
            sd_ref[...] = s0d_ref[0]
        else:
            sc_ref[...] = jnp.zeros_like(sc_ref)
            sd_ref[...] = jnp.zeros_like(sd_ref)

    nchunk = REC_BLOCK // CHUNK
    keys = [(d, h) for d in range(2) for h in range(REC_HEADS)]
    for step in range(nchunk):
        chunk_of = lambda d: step if d == 0 else nchunk - 1 - step

        def tok(d, idx, h):
            c = chunk_of(d)
            ref = ins[d][idx]
            sl = (slice(c * CHUNK, (c + 1) * CHUNK), slice(h * LANES, (h + 1) * LANES))
            return ref[(0,) * (len(ref.shape) - 2) + sl]

        def dec(d, idx, h):
            c = chunk_of(d)
            return ins[d][idx][0, 0, 0, c:c + 1, h * LANES:(h + 1) * LANES]

        def out_slice(d, h):
            c = chunk_of(d)
            return (0, slice(c * CHUNK, (c + 1) * CHUNK), slice(h * LANES, (h + 1) * LANES))

        S = {k: sd_ref[k[0], k[1]] for k in keys}
        Sb = {k: S[k].astype(BF16) for k in keys}
        ws = {(d, h): _dot(tok(d, 5, h), Sb[d, h]) for d, h in keys}
        ST = {k: sc_ref[k[0], k[1]] for k in keys}
        if with_output:
            for d, h in keys:
                oc_refs[d][out_slice(d, h)] = tok(d, 2, h) + _dot_nt(tok(d, 0, h), ST[d, h].astype(BF16))
            qs = {(d, h): _dot(tok(d, 8, h), Sb[d, h]) for d, h in keys}
        for d, h in keys:
            sc_ref[d, h] = ST[d, h] * dec(d, 3, h) + _dot_tn(tok(d, 4, h), tok(d, 1, h))
        vnb = {(d, h): (tok(d, 6, h).astype(F32) - ws[d, h]).astype(BF16) for d, h in keys}
        for d, h in keys:
            sd_ref[d, h] = S[d, h] * dec(d, 10, h) + _dot_tn(tok(d, 7, h), vnb[d, h])
        if with_output:
            for d, h in keys:
                vpair = jnp.concatenate([vnb[d, h], vnb[d, h]], axis=0)
                od_refs[d][out_slice(d, h)] = qs[d, h] + _dot(tok(d, 9, h), vpair)

    @pl.when(i == nblk - 1)
    def _():
        sfc_ref[0] = sc_ref[...]
        sfd_ref[0] = sd_ref[...]


def _rec_scan(prep, init, with_output):
    _, b, t, w = prep[0].shape
    nblk = t // REC_BLOCK
    n = REC_BLOCK
    in_specs, args = [], []
    for d in range(2):
        blk = (lambda i: i) if d == 0 else (lambda i: nblk - 1 - i)
        tok = pl.BlockSpec((1, 1, n, w), lambda bi, i, d=d, blk=blk: (d, bi, blk(i), 0))
        dec = pl.BlockSpec((1, 1, 1, 8, w), lambda bi, i, d=d, blk=blk: (d, bi, blk(i), 0, 0))
        cvs = pl.BlockSpec((1, n, w), lambda bi, i, blk=blk: (bi, blk(i), 0))
        in_specs += [tok, tok, tok, dec, cvs, tok, tok, tok, tok, tok, dec]
        args += list(prep)
    state = pl.BlockSpec((1, 2, REC_HEADS, LANES, LANES), lambda bi, i: (bi, 0, 0, 0, 0))
    if init is not None:
        in_specs += [state, state]
        args += list(init)
    out_specs, out_shape = [], []
    if with_output:
        fwd = pl.BlockSpec((1, n, w), lambda bi, i: (bi, i, 0))
        bwd = pl.BlockSpec((1, n, w), lambda bi, i: (bi, nblk - 1 - i, 0))
        out_specs += [fwd, bwd, fwd, bwd]
        out_shape += [jax.ShapeDtypeStruct((b, t, w), F32)] * 4
    out_specs += [state, state]
    out_shape += [jax.ShapeDtypeStruct((b, 2, REC_HEADS, LANES, LANES), F32)] * 2
    res = pl.pallas_call(
        functools.partial(_rec_scan_kernel, nblk=nblk, has_init=init is not None, with_output=with_output),
        grid=(b, nblk),
        in_specs=in_specs,
        out_specs=out_specs,
        out_shape=out_shape,
        scratch_shapes=[pltpu.VMEM((2, REC_HEADS, LANES, LANES), F32)] * 2,
        compiler_params=_params("parallel", "arbitrary"),
        name="rec_scan",
    )(*args)
    return (res[:4], (res[4], res[5])) if with_output else (None, (res[0], res[1]))


def _rec_out_kernel(x_ref, ocf_ref, ocb_ref, odf_ref, odb_ref, cg_ref, dg_ref, cn_ref, dn_ref, wa_ref, wb_ref,
                    g_ref, o_ref):
    def gated(f_ref, b_ref, gate_ref, nw_ref):
        o = f_ref[0] + b_ref[0]
        gate = _silu(gate_ref[0].astype(F32))
        parts = []
        for h in range(REC_HEADS):
            oh = o[:, h * LANES:(h + 1) * LANES]
            parts.append(oh * lax.rsqrt(jnp.mean(oh * oh, axis=-1, keepdims=True) + EPS) * nw_ref[...])
        return (jnp.concatenate(parts, axis=1) * gate).astype(BF16)

    y = (_dot(gated(ocf_ref, ocb_ref, cg_ref, cn_ref), wa_ref[...])
         + _dot(gated(odf_ref, odb_ref, dg_ref, dn_ref), wb_ref[...]))
    o_ref[0] = x_ref[0] + g_ref[0] * y


def _rec_out(x, outs, p16, c_norm, d_norm, wa, wb, gate, tm):
    b, t, d = x.shape
    w = REC_W
    tokw = pl.BlockSpec((1, tm, w), lambda bi, i: (bi, i, 0))
    return pl.pallas_call(
        _rec_out_kernel,
        grid=(b, t // tm),
        in_specs=[
            pl.BlockSpec((1, tm, d), lambda bi, i: (bi, i, 0)),
            tokw, tokw, tokw, tokw,
            pl.BlockSpec((1, tm, w), lambda bi, i: (bi, i, P16_HGATE)),
            pl.BlockSpec((1, tm, w), lambda bi, i: (bi, i, P16_DGATE)),
            pl.BlockSpec((1, LANES), lambda bi, i: (0, 0)),
            pl.BlockSpec((1, LANES), lambda bi, i: (0, 0)),
            pl.BlockSpec((w, d), lambda bi, i: (0, 0)),
            pl.BlockSpec((w, d), lambda bi, i: (0, 0)),
            _mod_spec(gate, d),
        ],
        out_specs=pl.BlockSpec((1, tm, d), lambda bi, i: (bi, i, 0)),
        out_shape=jax.ShapeDtypeStruct((b, t, d), F32),
        compiler_params=_params("parallel", "parallel"),
        name="rec_out",
    )(x, *outs, p16, p16, c_norm.reshape(1, LANES), d_norm.reshape(1, LANES), wa, wb, gate)


def _pad_rows(a, rows):
    return jnp.pad(a, ((0, rows - a.shape[0]),) + ((0, 0),) * (a.ndim - 1))


def kernel(x, c, ctx, c_ctx, ada_w, ada_b, norm_mix, norm_ffn, w_mix_out, ffn_gate, ffn_up, ffn_down,
           att_w_in, att_rpb, att_lambda, att_subln, rec_w_in, rec_lb_logits, rec_conv_w, rec_a_log,
           rec_dt_bias, rec_c_norm, rec_d_norm, final_norm):
    b, t, d = x.shape
    n_ctx = ctx.shape[1]
    depth = ada_w.shape[0]
    assert depth == 2 and t % (8 * GRID_W) == 0 and n_ctx % REC_BLOCK == 0
    tm = 512 if t % 512 == 0 else 256
    tmc = 256

    nrow = -(-(b + 1) // 8) * 8
    mods = _adaln(_pad_rows(jnp.concatenate([c, c_ctx[None, :]], axis=0).astype(F32), nrow), ada_w, ada_b)

    def layer_mods(layer):
        m = mods[layer].reshape(nrow, 6, 1, d)
        return [m[:b, k] for k in range(6)], [m[b:b + 1, k] for k in range(6)]

    bf = lambda a: a.astype(BF16)
    h_ctx = ctx

    ml, mc = layer_mods(0)
    na_w, df_w = NA_HEADS * NA_HEAD_DIM, 2 * DIFF_HEADS * DIFF_HEAD_DIM
    qscale = jnp.ones((att_w_in.shape[2],), F32)
    qscale = qscale.at[:na_w].set(NA_HEAD_DIM ** -0.5 * LOG2E).at[3 * na_w:3 * na_w + df_w].set(DIFF_HEAD_DIM ** -0.5 * LOG2E)
    w_in = bf(att_w_in[0] * qscale[None, :])
    rope = _rope_tables(t)
    qkv = _proj(x, norm_mix[0], ml[0], ml[1], w_in, BF16, tm, 512, rope=rope, rope_tiles=(3, 4))
    qkv_c = _proj(h_ctx, norm_mix[0], mc[0], mc[1], w_in, BF16, tmc, 512)
    lam_init = 0.8 - 0.6 * math.exp(-0.3 * 0)
    tb = _na_bias_table(att_rpb[0])
    y_na = _na_attention(qkv, qkv_c, tb)
    y_df = _diff_attention(qkv, qkv_c, qkv, att_lambda[0], att_subln[0], lam_init, tq=512, tk=min(1024, t), nstream=2)
    yc_na = _ctx_na_attention(qkv_c)
    yc_df = _diff_attention(qkv_c, qkv_c, None, att_lambda[0], att_subln[0], lam_init, tq=n_ctx, tk=0)
    wo = bf(w_mix_out[0])
    wg, wu, wd = bf(ffn_gate[0]), bf(ffn_up[0]), bf(ffn_down[0])
    x = _out_proj(x, y_na, y_df, wo[:na_w], wo[na_w:], ml[2], tm)
    x = _ffn(x, norm_ffn[0], ml[3], ml[4], ml[5], wg, wu, wd, None, tm)
    h_ctx = _out_proj(h_ctx, yc_na, yc_df, wo[:na_w], wo[na_w:], mc[2], tmc)
    h_ctx = _ffn(h_ctx, norm_ffn[0], mc[3], mc[4], mc[5], wg, wu, wd, None, tmc)

    ml, mc = layer_mods(1)
    hw = REC_W
    wr = rec_w_in[0]
    cuts = [0, hw, 2 * hw, 3 * hw, 4 * hw, 5 * hw, 8 * hw, 8 * hw + 4 * REC_HEADS, 9 * hw + 4 * REC_HEADS]
    hq, hff, hfb, hi, hgate, dqkv, dab, dgate = [wr[:, cuts[k]:cuts[k + 1]] for k in range(8)]
    w16 = bf(jnp.concatenate([dqkv, hq, hi, hgate, dgate], axis=1))
    w32 = bf(jnp.concatenate([hff, hfb, jnp.pad(dab, ((0, 0), (0, LANES - dab.shape[1])))], axis=1))
    lbs = jax.nn.softmax(rec_lb_logits.astype(F32), axis=0)
    lb = _pad_rows((jnp.cumsum(lbs, axis=0) - lbs[0])[1], 8)
    conv_w = _pad_rows(rec_conv_w[0].astype(F32), 8)
    a_rate = jnp.exp(rec_a_log[0].astype(F32)).reshape(-1)
    dtb = rec_dt_bias[0].astype(F32).reshape(-1)
    ab = jnp.zeros((8, LANES), F32).at[0, :a_rate.shape[0]].set(a_rate).at[1, :dtb.shape[0]].set(dtb)

    def rec_inputs(h, mods_, tile):
        p16 = _proj(h, norm_mix[1], mods_[0], mods_[1], w16, BF16, tile, 512)
        p32 = _proj(h, norm_mix[1], mods_[0], mods_[1], w32, F32, tile, w32.shape[1])
        return p16, _rec_prep(p16, p32, conv_w, lb, ab)

    _, prep_c = rec_inputs(h_ctx, mc, tmc)
    p16, prep_l = rec_inputs(x, ml, tm)
    _, s_ctx = _rec_scan(prep_c, None, with_output=False)
    outs, _ = _rec_scan(prep_l, s_ctx, with_output=True)
    wo = bf(w_mix_out[1])
    x = _rec_out(x, outs, p16, rec_c_norm[0].astype(F32), rec_d_norm[0].astype(F32), wo[:hw], wo[hw:], ml[2], tm)
    return _ffn(x, norm_ffn[1], ml[3], ml[4], ml[5], bf(ffn_gate[1]), bf(ffn_up[1]), bf(ffn_down[1]), final_norm, tm)
```

```python
import functools
import math

import jax
import jax.numpy as jnp
from jax import lax
from jax.experimental import pallas as pl
from jax.experimental.pallas import tpu as pltpu

F32 = jnp.float32
BF16 = jnp.bfloat16

EPS = 1e-6
MASK_VALUE = -1e30
GRID_W = 64
NA_HEADS = 8
NA_HEAD_DIM = 64
NA_WIN_ROWS = 8
NA_WIN_COLS = 16
DIFF_HEADS = 4
DIFF_HEAD_DIM = 64
REC_HEADS = 4
REC_HEAD_DIM = 128
CHUNK = 64
ROPE_THETA = 10000.0
LOG2E = math.log2(math.e)
LANES = 128
BF16_ROWS = 16
PAIR = 2 * CHUNK
REC_BLOCK = 4 * CHUNK
REC_W = REC_HEADS * REC_HEAD_DIM
VMEM_LIMIT = 56 * 1024 * 1024

P16_DQ, P16_DK, P16_DV, P16_HQ, P16_HI, P16_HGATE, P16_DGATE = range(7)
P32_HFF, P32_HFB = 0, 1
P32_DAB_TILE = 2 * REC_HEADS

_NT = (((1,), (1,)), ((), ()))
_TN = (((0,), (0,)), ((), ()))


def _params(*sem):
    return pltpu.CompilerParams(dimension_semantics=sem, vmem_limit_bytes=VMEM_LIMIT)


def _resident(shape):
    return pl.BlockSpec(shape, lambda *_: (0,) * len(shape), pipeline_mode=pl.Buffered(1))


def _dot(a, b):
    return jnp.dot(a, b, preferred_element_type=F32)


def _dot_nt(a, b):
    return lax.dot_general(a, b, _NT, preferred_element_type=F32)


def _dot_tn(a, b):
    return lax.dot_general(a, b, _TN, preferred_element_type=F32)


def _mask01(m):
    return jnp.where(m, 1.0, 0.0).astype(BF16)


def _sigmoid(x):
    return 1.0 / (1.0 + jnp.exp(-x))


def _silu(x):
    return x * _sigmoid(x)


def _rms_mod(x, nw, shift, scale):
    y = x * lax.rsqrt(jnp.mean(x * x, axis=-1, keepdims=True) + EPS)
    return (y * nw) * (1.0 + scale) + shift


def _mod_spec(arr, d):
    per_batch = arr.shape[0] > 1
    return pl.BlockSpec((1, 1, d), (lambda bi, i: (bi, 0, 0)) if per_batch else (lambda bi, i: (0, 0, 0)))


def _adaln_kernel(c_ref, w_ref, b_ref, o_ref):
    a = _silu(c_ref[...])
    w = w_ref[0]
    a1 = a.astype(BF16)
    a2 = (a - a1.astype(F32)).astype(BF16)
    w1 = w.astype(BF16)
    w2 = (w - w1.astype(F32)).astype(BF16)
    o_ref[0] = _dot(a1, w1) + _dot(a1, w2) + _dot(a2, w1) + b_ref[0]


def _adaln(cvecs, ada_w, ada_b):
    depth, d, n = ada_w.shape
    rows = cvecs.shape[0]
    tn = 1536
    return pl.pallas_call(
        _adaln_kernel,
        grid=(depth, n // tn),
        in_specs=[
            pl.BlockSpec((rows, d), lambda l, j: (0, 0)),
            pl.BlockSpec((1, d, tn), lambda l, j: (l, 0, j)),
            pl.BlockSpec((1, 1, tn), lambda l, j: (l, 0, j)),
        ],
        out_specs=pl.BlockSpec((1, rows, tn), lambda l, j: (l, 0, j)),
        out_shape=jax.ShapeDtypeStruct((depth, rows, n), F32),
        compiler_params=_params("parallel", "parallel"),
        name="adaln",
    )(cvecs, ada_w, ada_b.reshape(depth, 1, n))


def _proj_kernel(x_ref, nw_ref, sh_ref, sc_ref, w_ref, *rest, tn, rope_tiles):
    if rope_tiles:
        cos_ref, sa_ref, sb_ref, o_ref = rest
    else:
        (o_ref,) = rest
    h = _rms_mod(x_ref[0], nw_ref[...], sh_ref[0], sc_ref[0]).astype(BF16)
    n = w_ref.shape[1]
    for j in range(n // tn):
        cols = slice(j * tn, (j + 1) * tn)
        acc = _dot(h, w_ref[:, cols])
        if j in rope_tiles:
            tile = lambda r: jnp.concatenate([r[...]] * (tn // LANES), axis=1)
            acc = (acc * tile(cos_ref)
                   + pltpu.roll(acc, 16, 1) * tile(sa_ref)
                   + pltpu.roll(acc, tn - 16, 1) * tile(sb_ref))
        o_ref[0, :, cols] = acc.astype(o_ref.dtype)


def _proj(x, nw, shift, scale, w, out_dtype, tm, tn, rope=None, rope_tiles=()):
    b, t, d = x.shape
    n = w.shape[1]
    in_specs = [
        pl.BlockSpec((1, tm, d), lambda bi, i: (bi, i, 0)),
        pl.BlockSpec((1, d), lambda bi, i: (0, 0)),
        _mod_spec(shift, d), _mod_spec(scale, d),
        _resident((d, n)),
    ]
    args = [x, nw.reshape(1, d), shift, scale, w]
    if rope_tiles:
        in_specs += [pl.BlockSpec((tm, LANES), lambda bi, i: (i, 0))] * 3
        args += list(rope)
    return pl.pallas_call(
        functools.partial(_proj_kernel, tn=tn, rope_tiles=tuple(rope_tiles)),
        grid=(b, t // tm),
        in_specs=in_specs,
        out_specs=pl.BlockSpec((1, tm, n), lambda bi, i: (bi, i, 0)),
        out_shape=jax.ShapeDtypeStruct((b, t, n), out_dtype),
        compiler_params=_params("parallel", "parallel"),
        name="norm_proj",
    )(*args)


def _rope_tables(t):
    pos = jnp.arange(t)
    rows, cols = (pos // GRID_W).astype(F32), (pos % GRID_W).astype(F32)
    quarter = DIFF_HEAD_DIM // 4
    inv_freq = ROPE_THETA ** (-jnp.arange(quarter, dtype=F32) / quarter)
    lane = jnp.arange(LANES)
    within = lane % DIFF_HEAD_DIM
    use_row = within < DIFF_HEAD_DIM // 2
    freq = inv_freq[within % quarter]
    ang = jnp.where(use_row[None, :], rows[:, None], cols[:, None]) * freq[None, :]
    cos, sin = jnp.cos(ang), jnp.sin(ang)
    second = ((within // quarter) % 2) == 1
    sa = jnp.where(second[None, :], sin, 0.0)
    sb = jnp.where(second[None, :], 0.0, -sin)
    return cos, sa, sb


def _na_bias_table(rpb):
    kw = NA_WIN_COLS
    col = jnp.arange(GRID_W)
    col_start = jnp.clip(col - kw // 2, 0, GRID_W - kw)
    col_ok = (col[None, :] >= col_start[:, None]) & (col[None, :] < col_start[:, None] + kw)
    col_idx = jnp.clip(col[None, :] - col[:, None] + kw - 1, 0, 2 * kw - 2)
    rpb_cols = jnp.where(col_ok[None, None], rpb.astype(F32)[:, :, col_idx], MASK_VALUE)
    idx = jnp.arange(NA_WIN_ROWS)[:, None] + jnp.arange(NA_WIN_ROWS)[None, :]
    tb = rpb_cols[:, idx] * LOG2E
    nh = rpb.shape[0]
    tb = tb.reshape(nh // 2, 2, NA_WIN_ROWS, NA_WIN_ROWS, GRID_W, GRID_W).transpose(0, 2, 1, 4, 3, 5)
    return tb.reshape(nh // 2, NA_WIN_ROWS, 2 * GRID_W, NA_WIN_ROWS * GRID_W)


def _softmax_pv(s_list, v_list):
    m = functools.reduce(jnp.maximum, [s.max(axis=-1, keepdims=True) for s in s_list])
    es = [jnp.exp2(s - m) for s in s_list]
    l = functools.reduce(jnp.add, [e.sum(axis=-1, keepdims=True) for e in es])
    o = functools.reduce(jnp.add, [_dot(e.astype(BF16), v) for e, v in zip(es, v_list)])
    return o / l


def _stack_heads(q, lo):
    return jnp.concatenate([q * _mask01(lo), q * _mask01(jnp.logical_not(lo))], axis=0)


def _na_kernel(q_ref, k_ref, v_ref, kc_ref, vc_ref, tb_ref, o_ref, *, rows, rows_per_step):
    i = pl.program_id(2)
    lo = lax.broadcasted_iota(jnp.int32, (GRID_W, LANES), 1) < NA_HEAD_DIM
    kc, vc = kc_ref[0], vc_ref[0]
    win = NA_WIN_ROWS * GRID_W

    starts, scores = [], []
    for qa in range(rows_per_step):
        r = i * rows_per_step + qa
        r0 = jnp.clip(r - NA_WIN_ROWS // 2, 0, rows - NA_WIN_ROWS)
        off = r0 - r + NA_WIN_ROWS - 1
        qs = _stack_heads(q_ref[0, qa * GRID_W:(qa + 1) * GRID_W, :], lo)
        start = pl.multiple_of(r0 * GRID_W, GRID_W)
        starts.append(start)
        scores.append([_dot_nt(qs, k_ref[0, pl.ds(start, win), :]) + tb_ref[0, off], _dot_nt(qs, kc)])
    for qa in range(rows_per_step):
        o = _softmax_pv(scores[qa], [v_ref[0, pl.ds(starts[qa], win), :], vc])
        o = jnp.where(lo, o[:GRID_W], o[GRID_W:])
        o_ref[0, qa * GRID_W:(qa + 1) * GRID_W, :] = o.astype(o_ref.dtype)


def _na_attention(qkv, qkv_c, tb):
    b, t, _ = qkv.shape
    ctx = qkv_c.shape[1]
    rows = t // GRID_W
    rps = 8
    hp = NA_HEADS // 2
    tq = rps * GRID_W
    return pl.pallas_call(
        functools.partial(_na_kernel, rows=rows, rows_per_step=rps),
        grid=(b, hp, rows // rps),
        in_specs=[
            pl.BlockSpec((1, tq, LANES), lambda bi, h, i: (bi, i, h)),
            pl.BlockSpec((1, t, LANES), lambda bi, h, i: (bi, 0, hp + h)),
            pl.BlockSpec((1, t, LANES), lambda bi, h, i: (bi, 0, 2 * hp + h)),
            pl.BlockSpec((1, ctx, LANES), lambda bi, h, i: (bi, 0, hp + h)),
            pl.BlockSpec((1, ctx, LANES), lambda bi, h, i: (bi, 0, 2 * hp + h)),
            pl.BlockSpec((1, NA_WIN_ROWS, 2 * GRID_W, NA_WIN_ROWS * GRID_W), lambda bi, h, i: (h, 0, 0, 0)),
        ],
        out_specs=pl.BlockSpec((1, tq, LANES), lambda bi, h, i: (bi, i, h)),
        out_shape=jax.ShapeDtypeStruct((b, t, hp * LANES), BF16),
        compiler_params=_params("parallel", "parallel", "arbitrary"),
        name="na_attention",
    )(qkv, qkv, qkv, qkv_c, qkv_c, tb)


def _ctx_na_kernel(q_ref, k_ref, v_ref, o_ref):
    n = q_ref.shape[1]
    lo = lax.broadcasted_iota(jnp.int32, (n, LANES), 1) < NA_HEAD_DIM
    o = _softmax_pv([_dot_nt(_stack_heads(q_ref[0], lo), k_ref[0])], [v_ref[0]])
    o_ref[0] = jnp.where(lo, o[:n], o[n:]).astype(o_ref.dtype)


def _ctx_na_attention(qkv_c):
    b, ctx, _ = qkv_c.shape
    hp = NA_HEADS // 2
    return pl.pallas_call(
        _ctx_na_kernel,
        grid=(b, hp),
        in_specs=[
            pl.BlockSpec((1, ctx, LANES), lambda bi, h: (bi, 0, h)),
            pl.BlockSpec((1, ctx, LANES), lambda bi, h: (bi, 0, hp + h)),
            pl.BlockSpec((1, ctx, LANES), lambda bi, h: (bi, 0, 2 * hp + h)),
        ],
        out_specs=pl.BlockSpec((1, ctx, LANES), lambda bi, h: (bi, 0, h)),
        out_shape=jax.ShapeDtypeStruct((b, ctx, hp * LANES), BF16),
        compiler_params=_params("parallel", "parallel"),
        name="ctx_na_attention",
    )(qkv_c, qkv_c, qkv_c)


def _lam_full(lam_ref, lam_init):
    lv = lam_ref[...]
    s1 = jnp.sum(lv[0:1] * lv[1:2], axis=-1, keepdims=True)
    s2 = jnp.sum(lv[2:3] * lv[3:4], axis=-1, keepdims=True)
    return jnp.exp(s1) - jnp.exp(s2) + lam_init


def _diff_post(o, sub_ref, lam_init):
    y = o * lax.rsqrt(jnp.mean(o * o, axis=-1, keepdims=True) + EPS)
    return y * sub_ref[...] * (1.0 - lam_init)


def _diff_kernel(q_ref, kc_ref, vc_ref, *rest, tk, n_lat, lam_init, nstream):
    if n_lat:
        k_ref, v_ref, lam_ref, sub_ref, o_ref, vx_ref, sa_ref, sb_ref, vcx_ref, m_ref, acc_ref = rest
    else:
        lam_ref, sub_ref, o_ref, vcx_ref, m_ref, acc_ref = rest
    tq = q_ref.shape[1] // nstream
    lo = lax.broadcasted_iota(jnp.int32, (tq, LANES), 1) < DIFF_HEAD_DIM
    qs = jnp.concatenate([_stack_heads(q_ref[0, s * tq:(s + 1) * tq, :], lo) for s in range(nstream)], axis=0)
    nrow = qs.shape[0]
    halves = [slice(j * nrow // 2, (j + 1) * nrow // 2) for j in range(2)]

    @pl.when(pl.program_id(2) == 0)
    def _():
        vcx_ref[:, :LANES] = vc_ref[0]
        vcx_ref[:, LANES:] = jnp.ones((vcx_ref.shape[0], LANES), BF16)
        if n_lat:
            vx_ref[:, :LANES] = v_ref[0]
            vx_ref[:, LANES:] = jnp.ones((vx_ref.shape[0], LANES), BF16)

    def absorb(s, vxb, rows, first=False):
        m_old = m_ref[rows, :]
        m_new = s.max(axis=-1, keepdims=True) if first else jnp.maximum(m_old, s.max(axis=-1, keepdims=True))
        pv = _dot(jnp.exp2((s - m_new).astype(BF16)), vxb)
        acc_ref[rows, :] = pv if first else jnp.exp2(m_old - m_new) * acc_ref[rows, :] + pv
        m_ref[rows, :] = m_new

    absorb(_dot_nt(qs, kc_ref[0]), vcx_ref[...], slice(0, nrow), first=True)
    if n_lat:
        def lat_k(c):
            return k_ref[0, pl.ds(pl.multiple_of(c * tk, tk), tk), :]

        def lat_vx(c):
            return vx_ref[pl.ds(pl.multiple_of(c * tk, tk), tk), :]

        def phase(cur_ref, nxt_ref, c, prefetch):
            if prefetch:
                nxt_ref[...] = _dot_nt(qs, lat_k(c + 1))
            for rows in halves:
                absorb(cur_ref[rows, :], lat_vx(c), rows)

        sa_ref[...] = _dot_nt(qs, lat_k(0))

        def body(j, carry):
            phase(sa_ref, sb_ref, 2 * j, True)
            phase(sb_ref, sa_ref, 2 * j + 1, True)
            return carry

        lax.fori_loop(0, n_lat // 2 - 1, body, 0)
        phase(sa_ref, sb_ref, n_lat - 2, True)
        phase(sb_ref, sa_ref, n_lat - 1, False)
    lam_full = _lam_full(lam_ref, lam_init)
    for s in range(nstream):
        acc = acc_ref[2 * s * tq:2 * (s + 1) * tq, :]
        o = acc[:, :LANES] / acc[:, LANES:LANES + 1]
        o = o[:tq] - lam_full * o[tq:]
        o_ref[0, s * tq:(s + 1) * tq, :] = _diff_post(o, sub_ref, lam_init).astype(o_ref.dtype)


def _diff_attention(q_src, qkv_c, qkv, lam, subln, lam_init, tq, tk, nstream=1):
    b, tqs, _ = q_src.shape
    ctx = qkv_c.shape[1]
    h0 = 3 * NA_HEADS // 2
    nh = DIFF_HEADS
    in_specs = [
        pl.BlockSpec((1, tq, LANES), lambda bi, h, i: (bi, i, h0 + h)),
        pl.BlockSpec((1, ctx, LANES), lambda bi, h, i: (bi, 0, h0 + nh + h)),
        pl.BlockSpec((1, ctx, LANES), lambda bi, h, i: (bi, 0, h0 + 2 * nh + h)),
    ]
    args = [q_src, qkv_c, qkv_c]
    n_lat = 0
    scratch = []
    if qkv is not None:
        t = qkv.shape[1]
        n_lat = t // tk
        in_specs += [
            pl.BlockSpec((1, t, LANES), lambda bi, h, i: (bi, 0, h0 + nh + h)),
            pl.BlockSpec((1, t, LANES), lambda bi, h, i: (bi, 0, h0 + 2 * nh + h)),
        ]
        args += [qkv, qkv]
        assert n_lat % 2 == 0
        scratch += [pltpu.VMEM((t, 2 * LANES), BF16), pltpu.VMEM((2 * tq, tk), F32), pltpu.VMEM((2 * tq, tk), F32)]
    scratch += [pltpu.VMEM((ctx, 2 * LANES), BF16), pltpu.VMEM((2 * tq, 1), F32), pltpu.VMEM((2 * tq, 2 * LANES), F32)]
    in_specs += [
        pl.BlockSpec(lam.shape, lambda bi, h, i: (0, 0)),
        pl.BlockSpec((1, LANES), lambda bi, h, i: (0, 0)),
    ]
    args += [lam.astype(F32), subln.astype(F32).reshape(1, LANES)]
    return pl.pallas_call(
        functools.partial(_diff_kernel, tk=tk, n_lat=n_lat, lam_init=lam_init, nstream=nstream),
        grid=(b, nh, tqs // tq),
        in_specs=in_specs,
        out_specs=pl.BlockSpec((1, tq, LANES), lambda bi, h, i: (bi, i, h)),
        out_shape=jax.ShapeDtypeStruct((b, tqs, nh * LANES), BF16),
        scratch_shapes=scratch,
        compiler_params=_params("parallel", "parallel", "arbitrary"),
        name="diff_attention",
    )(*args)


def _out_kernel(x_ref, ya_ref, yb_ref, wa_ref, wb_ref, g_ref, o_ref):
    y = _dot(ya_ref[0], wa_ref[...]) + _dot(yb_ref[0], wb_ref[...])
    o_ref[0] = x_ref[0] + g_ref[0] * y


def _out_proj(x, ya, yb, wa, wb, gate, tm):
    b, t, d = x.shape
    ka, kb = ya.shape[2], yb.shape[2]
    return pl.pallas_call(
        _out_kernel,
        grid=(b, t // tm),
        in_specs=[
            pl.BlockSpec((1, tm, d), lambda bi, i: (bi, i, 0)),
            pl.BlockSpec((1, tm, ka), lambda bi, i: (bi, i, 0)),
            pl.BlockSpec((1, tm, kb), lambda bi, i: (bi, i, 0)),
            pl.BlockSpec((ka, d), lambda bi, i: (0, 0)),
            pl.BlockSpec((kb, d), lambda bi, i: (0, 0)),
            _mod_spec(gate, d),
        ],
        out_specs=pl.BlockSpec((1, tm, d), lambda bi, i: (bi, i, 0)),
        out_shape=jax.ShapeDtypeStruct((b, t, d), F32),
        compiler_params=_params("parallel", "parallel"),
        name="out_proj",
    )(x, ya, yb, wa, wb, gate)


def _ffn_kernel(x_ref, nw_ref, sh_ref, sc_ref, g_ref, wg_ref, wu_ref, wd_ref, *rest, final):
    if final:
        fn_ref, o_ref = rest
    else:
        (o_ref,) = rest
    x = x_ref[0]
    h = _rms_mod(x, nw_ref[...], sh_ref[0], sc_ref[0]).astype(BF16)
    a = _dot(h, wg_ref[...])
    u = _dot(h, wu_ref[...])
    z = (_silu(a) * u).astype(BF16)
    out = x + g_ref[0] * _dot(z, wd_ref[...])
    if final:
        out = out * lax.rsqrt(jnp.mean(out * out, axis=-1, keepdims=True) + EPS) * fn_ref[...]
    o_ref[0] = out


def _ffn(x, nw, shift, scale, gate, wg, wu, wd, final_norm, tm):
    b, t, d = x.shape
    f = wg.shape[1]
    in_specs = [
        pl.BlockSpec((1, tm, d), lambda bi, i: (bi, i, 0)),
        pl.BlockSpec((1, d), lambda bi, i: (0, 0)),
        _mod_spec(shift, d), _mod_spec(scale, d), _mod_spec(gate, d),
        _resident((d, f)), _resident((d, f)), _resident((f, d)),
    ]
    args = [x, nw.reshape(1, d), shift, scale, gate, wg, wu, wd]
    if final_norm is not None:
        in_specs.append(pl.BlockSpec((1, d), lambda bi, i: (0, 0)))
        args.append(final_norm.reshape(1, d))
    return pl.pallas_call(
        functools.partial(_ffn_kernel, final=final_norm is not None),
        grid=(b, t // tm),
        in_specs=in_specs,
        out_specs=pl.BlockSpec((1, tm, d), lambda bi, i: (bi, i, 0)),
        out_shape=jax.ShapeDtypeStruct((b, t, d), F32),
        compiler_params=_params("parallel", "parallel"),
        name="ffn",
    )(*args)


def _chunk_masks(n):
    row = lax.broadcasted_iota(jnp.int32, (n, n), 0)
    col = lax.broadcasted_iota(jnp.int32, (n, n), 1)
    shift = int(math.log2(CHUNK))
    same = lax.shift_right_logical(row, shift) == lax.shift_right_logical(col, shift)
    return row, col, same


def _chunk_scan(x, reverse, pos):
    n, w = x.shape
    s = 1
    while s < 8:
        if reverse:
            x = x + jnp.where(pos < CHUNK - s, pltpu.roll(x, n - s, 0), 0.0)
        else:
            x = x + jnp.where(pos >= s, pltpu.roll(x, s, 0), 0.0)
        s *= 2
    while s < CHUNK:
        parts = []
        for c in range(n // CHUNK):
            xc = x[c * CHUNK:(c + 1) * CHUNK]
            pad = jnp.zeros((s, w), F32)
            parts.append(xc + (jnp.concatenate([xc[s:], pad], axis=0) if reverse
                               else jnp.concatenate([pad, xc[:CHUNK - s]], axis=0)))
        x = jnp.concatenate(parts, axis=0)
        s *= 2
    return x


def _chunk_totals(G, reverse):
    n, w = G.shape
    r8 = lax.broadcasted_iota(jnp.int32, (8, w), 0)
    parts, dec = [], jnp.zeros((8, w), F32)
    for c in range(n // CHUNK):
        r = c * CHUNK + (0 if reverse else CHUNK - 1)
        tot = G[r:r + 1, :]
        parts.append(jnp.broadcast_to(tot, (CHUNK, w)))
        dec = jnp.where(r8 == c, jnp.broadcast_to(tot, (8, w)), dec)
    return jnp.concatenate(parts, axis=0), dec


def _rec_prep_kernel(dq_ref, dk_ref, dv_ref, pq_ref, pk_ref, pv_ref, nq_ref, nk_ref, nv_ref,
                     hq_ref, hi_ref, hff_ref, hfb_ref, dab_ref, cwq_ref, cwk_ref, cwv_ref, lb_ref, ab_ref,
                     cqe_ref, ckt_ref, coi_ref, cdec_ref, cv_ref,
                     dw_ref, du_ref, dkt_ref, dqe_ref, daqk_ref, ddec_ref, *, nblk):
    i = pl.program_id(1)
    n = REC_BLOCK
    w = REC_W
    row, col, same = _chunk_masks(PAIR)
    incl = (same & (row >= col), same & (row <= col))
    strict = (same & (row > col), same & (row < col))
    rid = lax.broadcasted_iota(jnp.int32, (n, w), 0)
    pos = rid & (CHUNK - 1)
    pos1 = pos[:, :LANES]
    pairs = [slice(p * PAIR, (p + 1) * PAIR) for p in range(n // PAIR)]
    heads = [slice(h * LANES, (h + 1) * LANES) for h in range(REC_HEADS)]
    keys = [(h, d, p) for h in range(REC_HEADS) for d in range(2) for p in range(len(pairs))]
    hp_keys = [(h, p) for h in range(REC_HEADS) for p in range(len(pairs))]

    cq = _silu(hq_ref[0].astype(F32)) * (REC_HEAD_DIM ** -0.5)
    cvb = hi_ref[0]
    cv_ref[0] = cvb
    qin, ktail = {}, {}
    for d, f_ref in enumerate((hff_ref, hfb_ref)):
        lbv = lb_ref[pl.ds(d, 1), :]
        sig = _sigmoid(f_ref[0])
        g = jnp.log(lbv + (1.0 - lbv) * sig)
        ck = (1.0 - lbv) * (1.0 - sig)
        G = _chunk_scan(g, d == 1, pos)
        Gl, dec = _chunk_totals(G, d == 1)
        ktail[d] = (ck * jnp.exp(Gl - G)).astype(BF16)
        qin[d] = (cq * jnp.exp(G - Gl)).astype(BF16)
        cqe_ref[d, 0] = (cq * jnp.exp(G)).astype(BF16)
        ckt_ref[d, 0] = ktail[d]
        cdec_ref[d, 0, 0] = jnp.exp(dec)
    amat = {(h, d, p): jnp.where(incl[d], _dot_nt(qin[d][pairs[p], heads[h]], ktail[d][pairs[p], heads[h]]), 0.0)
            for h, d, p in keys}
    for h, d, p in keys:
        coi_ref[d, 0, pairs[p], heads[h]] = _dot(amat[h, d, p].astype(BF16), cvb[pairs[p], heads[h]])

    last = BF16_ROWS - 1

    def conv_silu(x_ref, p_ref, n_ref, cw_ref):
        x = x_ref[0].astype(F32)
        cw = cw_ref[...]
        prev_row = jnp.where(i > 0, p_ref[0, last:last + 1, :].astype(F32), 0.0)
        next_row = jnp.where(i < nblk - 1, n_ref[0, 0:1, :].astype(F32), 0.0)
        xp = jnp.where(rid == 0, prev_row, pltpu.roll(x, 1, 0))
        xn = jnp.where(rid == n - 1, next_row, pltpu.roll(x, n - 1, 0))
        return _silu(cw[0:1] * xp + cw[1:2] * x + cw[2:3] * xn)

    def l2n(x):
        return jnp.concatenate(
            [x[:, hs] * lax.rsqrt(jnp.sum(x[:, hs] * x[:, hs], axis=-1, keepdims=True) + EPS) for hs in heads], axis=1)

    qn = l2n(conv_silu(dq_ref, pq_ref, nq_ref, cwq_ref)) * (REC_HEAD_DIM ** -0.5)
    kn = l2n(conv_silu(dk_ref, pk_ref, nk_ref, cwk_ref))
    vv = conv_silu(dv_ref, pv_ref, nv_ref, cwv_ref)
    knb, qnb = kn.astype(BF16), qn.astype(BF16)
    kk = {(h, p): _dot_nt(knb[pairs[p], heads[h]], knb[pairs[p], heads[h]]) for h, p in hp_keys}
    qk = {(h, p): _dot_nt(qnb[pairs[p], heads[h]], knb[pairs[p], heads[h]]) for h, p in hp_keys}
    tile = dab_ref[0]
    arate, dtb = ab_ref[0:1, :], ab_ref[1:2, :]
    xs = tile + dtb
    gt = -arate * (jnp.maximum(xs, 0.0) + jnp.log(1.0 + jnp.exp(-jnp.abs(xs))))
    bt = _sigmoid(tile)
    lane = lax.broadcasted_iota(jnp.int32, (n, LANES), 1)
    spread = lambda t, j0: jnp.concatenate(
        [jnp.broadcast_to(jnp.sum(jnp.where(lane == j0 + h, t, 0.0), axis=-1, keepdims=True), (n, LANES))
         for h in range(REC_HEADS)], axis=1)
    eye = (row == col).astype(F32)
    rhs, gmat = {}, {}
    for d in range(2):
        gsc = _chunk_scan(gt, d == 1, pos1)
        Gc = spread(gsc, d * REC_HEADS)
        bb = spread(bt, (2 + d) * REC_HEADS)
        Gl, dec = _chunk_totals(Gc, d == 1)
        eG = jnp.exp(Gc)
        rhs[d] = ((kn * bb * eG).astype(BF16), (vv * bb).astype(BF16))
        for h, p in hp_keys:
            gcp = Gc[pairs[p], heads[h]]
            gamma = jnp.exp(jnp.where(incl[d], gcp - gcp.T, MASK_VALUE))
            gmat[h, d, p] = jnp.where(strict[d], bb[pairs[p], heads[h]] * kk[h, p] * gamma, 0.0)
            daqk_ref[d, 0, pairs[p], heads[h]] = jnp.where(incl[d], qk[h, p] * gamma, 0.0).astype(BF16)
        dkt_ref[d, 0] = (kn * jnp.exp(Gl - Gc)).astype(BF16)
        dqe_ref[d, 0] = (qn * eG).astype(BF16)
        ddec_ref[d, 0, 0] = jnp.exp(dec)

    T = {key: eye for key in keys}
    for k in range(int(math.log2(CHUNK))):
        blk = lax.shift_right_logical(row, k + 1) == lax.shift_right_logical(col, k + 1)
        sub = lax.shift_right_logical(row, k) != lax.shift_right_logical(col, k)
        ak = {key: jnp.where(blk & sub, gmat[key], 0.0) for key in keys}
        if k == 0:
            T = {key: T[key] - ak[key] for key in keys}
            continue
        Tb = {key: T[key].astype(BF16) for key in keys}
        x = {key: _dot(ak[key].astype(BF16), Tb[key]).astype(BF16) for key in keys}
        T = {key: T[key] - _dot(Tb[key], x[key]) for key in keys}
    for h, d, p in keys:
        both = jnp.concatenate([rhs[d][0][pairs[p], heads[h]], rhs[d][1][pairs[p], heads[h]]], axis=1)
        wu = _dot(T[h, d, p].astype(BF16), both)
        dw_ref[d, 0, pairs[p], heads[h]] = wu[:, :LANES].astype(BF16)
        du_ref[d, 0, pairs[p], heads[h]] = wu[:, LANES:].astype(BF16)


def _rec_prep(p16, p32, conv_w, lb, ab):
    b, t, _ = p16.shape
    nblk = t // REC_BLOCK
    n = REC_BLOCK
    w = REC_W
    per = n // BF16_ROWS
    main = lambda blk: pl.BlockSpec((1, n, w), lambda bi, i: (bi, i, blk))
    prev = lambda blk: pl.BlockSpec((1, BF16_ROWS, w), lambda bi, i: (bi, jnp.maximum(i * per - 1, 0), blk))
    nxt = lambda blk: pl.BlockSpec((1, BF16_ROWS, w),
                                   lambda bi, i: (bi, jnp.minimum((i + 1) * per, t // BF16_ROWS - 1), blk))
    taps = lambda blk: pl.BlockSpec((8, w), lambda bi, i: (0, blk))
    in_specs = [main(P16_DQ), main(P16_DK), main(P16_DV), prev(P16_DQ), prev(P16_DK), prev(P16_DV),
                nxt(P16_DQ), nxt(P16_DK), nxt(P16_DV), main(P16_HQ), main(P16_HI),
                main(P32_HFF), main(P32_HFB),
                pl.BlockSpec((1, n, LANES), lambda bi, i: (bi, i, P32_DAB_TILE)),
                taps(0), taps(1), taps(2),
                pl.BlockSpec((8, w), lambda bi, i: (0, 0)),
                pl.BlockSpec((8, LANES), lambda bi, i: (0, 0))]
    dir_tok = lambda dt: (jax.ShapeDtypeStruct((2, b, t, w), dt),
                          pl.BlockSpec((2, 1, n, w), lambda bi, i: (0, bi, i, 0)))
    dir_dec = (jax.ShapeDtypeStruct((2, b, nblk, 8, w), F32),
               pl.BlockSpec((2, 1, 1, 8, w), lambda bi, i: (0, bi, i, 0, 0)))
    outs = [dir_tok(BF16), dir_tok(BF16), dir_tok(F32), dir_dec,
            (jax.ShapeDtypeStruct((b, t, w), BF16), pl.BlockSpec((1, n, w), lambda bi, i: (bi, i, 0))),
            dir_tok(BF16), dir_tok(BF16), dir_tok(BF16), dir_tok(BF16), dir_tok(BF16), dir_dec]
    return pl.pallas_call(
        functools.partial(_rec_prep_kernel, nblk=nblk),
        grid=(b, nblk),
        in_specs=in_specs,
        out_specs=[o[1] for o in outs],
        out_shape=[o[0] for o in outs],
        compiler_params=_params("parallel", "arbitrary"),
        name="rec_prep",
    )(*([p16] * 11), p32, p32, p32, conv_w, conv_w, conv_w, lb, ab)


_N_PREP = 11


def _rec_scan_kernel(*refs, nblk, has_init, with_output):
    ins = [refs[d * _N_PREP:(d + 1) * _N_PREP] for d in range(2)]
    rest = list(refs[2 * _N_PREP:])
    s0c_ref, s0d_ref = (rest.pop(0), rest.pop(0)) if has_init else (None, None)
    if with_output:
        oc_refs, od_refs = (rest.pop(0), rest.pop(0)), (rest.pop(0), rest.pop(0))
    sfc_ref, sfd_ref, sc_ref, sd_ref = rest
    i = pl.program_id(1)

    @pl.when(i == 0)
    def _():
        if has_init:
            sc_ref[...] = s0c_ref[0]
            sd_ref[...] = s0d_ref[0]
        else:
            sc_ref[...] = jnp.zeros_like(sc_ref)
            sd_ref[...] = jnp.zeros_like(sd_ref)

    nchunk = REC_BLOCK // CHUNK
    keys = [(d, h) for d in range(2) for h in range(REC_HEADS)]
    for step in range(nchunk):
        chunk_of = lambda d: step if d == 0 else nchunk - 1 - step

        def tok(d, idx, h):
            c = chunk_of(d)
            ref = ins[d][idx]
            sl = (slice(c * CHUNK, (c + 1) * CHUNK), slice(h * LANES, (h + 1) * LANES))
            return ref[(0,) * (len(ref.shape) - 2) + sl]

        def dec(d, idx, h):
            c = chunk_of(d)
            return ins[d][idx][0, 0, 0, c:c + 1, h * LANES:(h + 1) * LANES]

        def out_slice(d, h):
            c = chunk_of(d)
            return (0, slice(c * CHUNK, (c + 1) * CHUNK), slice(h * LANES, (h + 1) * LANES))

        S = {k: sd_ref[k[0], k[1]] for k in keys}
        Sb = {k: S[k].astype(BF16) for k in keys}
        ws = {(d, h): _dot(tok(d, 5, h), Sb[d, h]) for d, h in keys}
        ST = {k: sc_ref[k[0], k[1]] for k in keys}
        if with_output:
            for d, h in keys:
                oc_refs[d][out_slice(d, h)] = tok(d, 2, h) + _dot_nt(tok(d, 0, h), ST[d, h].astype(BF16))
            qs = {(d, h): _dot(tok(d, 8, h), Sb[d, h]) for d, h in keys}
        for d, h in keys:
            sc_ref[d, h] = ST[d, h] * dec(d, 3, h) + _dot_tn(tok(d, 4, h), tok(d, 1, h))
        vnb = {(d, h): (tok(d, 6, h).astype(F32) - ws[d, h]).astype(BF16) for d, h in keys}
        for d, h in keys:
            sd_ref[d, h] = S[d, h] * dec(d, 10, h) + _dot_tn(tok(d, 7, h), vnb[d, h])
        if with_output:
            for d, h in keys:
                vpair = jnp.concatenate([vnb[d, h], vnb[d, h]], axis=0)
                od_refs[d][out_slice(d, h)] = qs[d, h] + _dot(tok(d, 9, h), vpair)

    @pl.when(i == nblk - 1)
    def _():
        sfc_ref[0] = sc_ref[...]
        sfd_ref[0] = sd_ref[...]


def _rec_scan(prep, init, with_output):
    _, b, t, w = prep[0].shape
    nblk = t // REC_BLOCK
    n = REC_BLOCK
    in_specs, args = [], []
    for d in range(2):
        blk = (lambda i: i) if d == 0 else (lambda i: nblk - 1 - i)
        tok = pl.BlockSpec((1, 1, n, w), lambda bi, i, d=d, blk=blk: (d, bi, blk(i), 0))
        dec = pl.BlockSpec((1, 1, 1, 8, w), lambda bi, i, d=d, blk=blk: (d, bi, blk(i), 0, 0))
        cvs = pl.BlockSpec((1, n, w), lambda bi, i, blk=blk: (bi, blk(i), 0))
        in_specs += [tok, tok, tok, dec, cvs, tok, tok, tok, tok, tok, dec]
        args += list(prep)
    state = pl.BlockSpec((1, 2, REC_HEADS, LANES, LANES), lambda bi, i: (bi, 0, 0, 0, 0))
    if init is not None:
        in_specs += [state, state]
        args += list(init)
    out_specs, out_shape = [], []
    if with_output:
        fwd = pl.BlockSpec((1, n, w), lambda bi, i: (bi, i, 0))
        bwd = pl.BlockSpec((1, n, w), lambda bi, i: (bi, nblk - 1 - i, 0))
        out_specs += [fwd, bwd, fwd, bwd]
        out_shape += [jax.ShapeDtypeStruct((b, t, w), F32)] * 4
    out_specs += [state, state]
    out_shape += [jax.ShapeDtypeStruct((b, 2, REC_HEADS, LANES, LANES), F32)] * 2
    res = pl.pallas_call(
        functools.partial(_rec_scan_kernel, nblk=nblk, has_init=init is not None, with_output=with_output),
        grid=(b, nblk),
        in_specs=in_specs,
        out_specs=out_specs,
        out_shape=out_shape,
        scratch_shapes=[pltpu.VMEM((2, REC_HEADS, LANES, LANES), F32)] * 2,
        compiler_params=_params("parallel", "arbitrary"),
        name="rec_scan",
    )(*args)
    return (res[:4], (res[4], res[5])) if with_output else (None, (res[0], res[1]))


def _rec_out_kernel(x_ref, ocf_ref, ocb_ref, odf_ref, odb_ref, cg_ref, dg_ref, cn_ref, dn_ref, wa_ref, wb_ref,
                    g_ref, o_ref):
    def gated(f_ref, b_ref, gate_ref, nw_ref):
        o = f_ref[0] + b_ref[0]
        gate = _silu(gate_ref[0].astype(F32))
        parts = []
        for h in range(REC_HEADS):
            oh = o[:, h * LANES:(h + 1) * LANES]
            parts.append(oh * lax.rsqrt(jnp.mean(oh * oh, axis=-1, keepdims=True) + EPS) * nw_ref[...])
        return (jnp.concatenate(parts, axis=1) * gate).astype(BF16)

    y = (_dot(gated(ocf_ref, ocb_ref, cg_ref, cn_ref), wa_ref[...])
         + _dot(gated(odf_ref, odb_ref, dg_ref, dn_ref), wb_ref[...]))
    o_ref[0] = x_ref[0] + g_ref[0] * y


def _rec_out(x, outs, p16, c_norm, d_norm, wa, wb, gate, tm):
    b, t, d = x.shape
    w = REC_W
    tokw = pl.BlockSpec((1, tm, w), lambda bi, i: (bi, i, 0))
    return pl.pallas_call(
        _rec_out_kernel,
        grid=(b, t // tm),
        in_specs=[
            pl.BlockSpec((1, tm, d), lambda bi, i: (bi, i, 0)),
            tokw, tokw, tokw, tokw,
            pl.BlockSpec((1, tm, w), lambda bi, i: (bi, i, P16_HGATE)),
            pl.BlockSpec((1, tm, w), lambda bi, i: (bi, i, P16_DGATE)),
            pl.BlockSpec((1, LANES), lambda bi, i: (0, 0)),
            pl.BlockSpec((1, LANES), lambda bi, i: (0, 0)),
            pl.BlockSpec((w, d), lambda bi, i: (0, 0)),
            pl.BlockSpec((w, d), lambda bi, i: (0, 0)),
            _mod_spec(gate, d),
        ],
        out_specs=pl.BlockSpec((1, tm, d), lambda bi, i: (bi, i, 0)),
        out_shape=jax.ShapeDtypeStruct((b, t, d), F32),
        compiler_params=_params("parallel", "parallel"),
        name="rec_out",
    )(x, *outs, p16, p16, c_norm.reshape(1, LANES), d_norm.reshape(1, LANES), wa, wb, gate)


def _pad_rows(a, rows):
    return jnp.pad(a, ((0, rows - a.shape[0]),) + ((0, 0),) * (a.ndim - 1))


def kernel(x, c, ctx, c_ctx, ada_w, ada_b, norm_mix, norm_ffn, w_mix_out, ffn_gate, ffn_up, ffn_down,
           att_w_in, att_rpb, att_lambda, att_subln, rec_w_in, rec_lb_logits, rec_conv_w, rec_a_log,
           rec_dt_bias, rec_c_norm, rec_d_norm, final_norm):
    b, t, d = x.shape
    n_ctx = ctx.shape[1]
    depth = ada_w.shape[0]
    assert depth == 2 and t % (8 * GRID_W) == 0 and n_ctx % REC_BLOCK == 0
    tm = 512 if t % 512 == 0 else 256
    tmc = 256

    nrow = -(-(b + 1) // 8) * 8
    mods = _adaln(_pad_rows(jnp.concatenate([c, c_ctx[None, :]], axis=0).astype(F32), nrow), ada_w, ada_b)

    def layer_mods(layer):
        m = mods[layer].reshape(nrow, 6, 1, d)
        return [m[:b, k] for k in range(6)], [m[b:b + 1, k] for k in range(6)]

    bf = lambda a: a.astype(BF16)
    h_ctx = ctx

    ml, mc = layer_mods(0)
    na_w, df_w = NA_HEADS * NA_HEAD_DIM, 2 * DIFF_HEADS * DIFF_HEAD_DIM
    qscale = jnp.ones((att_w_in.shape[2],), F32)
    qscale = qscale.at[:na_w].set(NA_HEAD_DIM ** -0.5 * LOG2E).at[3 * na_w:3 * na_w + df_w].set(DIFF_HEAD_DIM ** -0.5 * LOG2E)
    w_in = bf(att_w_in[0] * qscale[None, :])
    rope = _rope_tables(t)
    qkv = _proj(x, norm_mix[0], ml[0], ml[1], w_in, BF16, tm, 512, rope=rope, rope_tiles=(3, 4))
    qkv_c = _proj(h_ctx, norm_mix[0], mc[0], mc[1], w_in, BF16, tmc, 512)
    lam_init = 0.8 - 0.6 * math.exp(-0.3 * 0)
    tb = _na_bias_table(att_rpb[0])
    y_na = _na_attention(qkv, qkv_c, tb)
    y_df = _diff_attention(qkv, qkv_c, qkv, att_lambda[0], att_subln[0], lam_init, tq=1024, tk=min(1024, t // 2), nstream=4)
    yc_na = _ctx_na_attention(qkv_c)
    yc_df = _diff_attention(qkv_c, qkv_c, None, att_lambda[0], att_subln[0], lam_init, tq=n_ctx, tk=0)
    wo = bf(w_mix_out[0])
    wg, wu, wd = bf(ffn_gate[0]), bf(ffn_up[0]), bf(ffn_down[0])
    x = _out_proj(x, y_na, y_df, wo[:na_w], wo[na_w:], ml[2], tm)
    x = _ffn(x, norm_ffn[0], ml[3], ml[4], ml[5], wg, wu, wd, None, tm)
    h_ctx = _out_proj(h_ctx, yc_na, yc_df, wo[:na_w], wo[na_w:], mc[2], tmc)
    h_ctx = _ffn(h_ctx, norm_ffn[0], mc[3], mc[4], mc[5], wg, wu, wd, None, tmc)

    ml, mc = layer_mods(1)
    hw = REC_W
    wr = rec_w_in[0]
    cuts = [0, hw, 2 * hw, 3 * hw, 4 * hw, 5 * hw, 8 * hw, 8 * hw + 4 * REC_HEADS, 9 * hw + 4 * REC_HEADS]
    hq, hff, hfb, hi, hgate, dqkv, dab, dgate = [wr[:, cuts[k]:cuts[k + 1]] for k in range(8)]
    w16 = bf(jnp.concatenate([dqkv, hq, hi, hgate, dgate], axis=1))
    w32 = bf(jnp.concatenate([hff, hfb, jnp.pad(dab, ((0, 0), (0, LANES - dab.shape[1])))], axis=1))
    lbs = jax.nn.softmax(rec_lb_logits.astype(F32), axis=0)
    lb = _pad_rows((jnp.cumsum(lbs, axis=0) - lbs[0])[1], 8)
    conv_w = _pad_rows(rec_conv_w[0].astype(F32), 8)
    a_rate = jnp.exp(rec_a_log[0].astype(F32)).reshape(-1)
    dtb = rec_dt_bias[0].astype(F32).reshape(-1)
    ab = jnp.zeros((8, LANES), F32).at[0, :a_rate.shape[0]].set(a_rate).at[1, :dtb.shape[0]].set(dtb)

    def rec_inputs(h, mods_, tile):
        p16 = _proj(h, norm_mix[1], mods_[0], mods_[1], w16, BF16, tile, 512)
        p32 = _proj(h, norm_mix[1], mods_[0], mods_[1], w32, F32, tile, w32.shape[1])
        return p16, _rec_prep(p16, p32, conv_w, lb, ab)

    _, prep_c = rec_inputs(h_ctx, mc, tmc)
    p16, prep_l = rec_inputs(x, ml, tm)
    _, s_ctx = _rec_scan(prep_c, None, with_output=False)
    outs, _ = _rec_scan(prep_l, s_ctx, with_output=True)
    wo = bf(w_mix_out[1])
    x = _rec_out(x, outs, p16, rec_c_norm[0].astype(F32), rec_d_norm[0].astype(F32), wo[:hw], wo[hw:], ml[2], tm)
    return _ffn(x, norm_ffn[1], ml[3], ml[4], ml[5], bf(ffn_gate[1]), bf(ffn_up[1]), bf(ffn_down[1]), final_norm, tm)
```

```python
import functools
import math

import jax
import jax.numpy as jnp
from jax import lax
from jax.experimental import pallas as pl
from jax.experimental.pallas import tpu as pltpu

F32 = jnp.float32
BF16 = jnp.bfloat16

EPS = 1e-6
MASK_VALUE = -1e30
GRID_W = 64
NA_HEADS = 8
NA_HEAD_DIM = 64
NA_WIN_ROWS = 8
NA_WIN_COLS = 16
DIFF_HEADS = 4
DIFF_HEAD_DIM = 64
REC_HEADS = 4
REC_HEAD_DIM = 128
CHUNK = 64
ROPE_THETA = 10000.0
LOG2E = math.log2(math.e)
LANES = 128
BF16_ROWS = 16
PAIR = 2 * CHUNK
REC_BLOCK = 4 * CHUNK
REC_W = REC_HEADS * REC_HEAD_DIM
VMEM_LIMIT = 56 * 1024 * 1024

P16_DQ, P16_DK, P16_DV, P16_HQ, P16_HI, P16_HGATE, P16_DGATE = range(7)
P32_HFF, P32_HFB = 0, 1
P32_DAB_TILE = 2 * REC_HEADS

_NT = (((1,), (1,)), ((), ()))
_TN = (((0,), (0,)), ((), ()))


def _params(*sem):
    return pltpu.CompilerParams(dimension_semantics=sem, vmem_limit_bytes=VMEM_LIMIT)


def _resident(shape):
    return pl.BlockSpec(shape, lambda *_: (0,) * len(shape), pipeline_mode=pl.Buffered(1))


def _dot(a, b):
    return jnp.dot(a, b, preferred_element_type=F32)


def _dot_nt(a, b):
    return lax.dot_general(a, b, _NT, preferred_element_type=F32)


def _dot_tn(a, b):
    return lax.dot_general(a, b, _TN, preferred_element_type=F32)


def _mask01(m):
    return jnp.where(m, 1.0, 0.0).astype(BF16)


def _sigmoid(x):
    return 0.5 * jnp.tanh(0.5 * x) + 0.5


def _silu(x):
    return x * _sigmoid(x)


def _rms_mod(x, nw, shift, scale):
    y = x * lax.rsqrt(jnp.mean(x * x, axis=-1, keepdims=True) + EPS)
    return (y * nw) * (1.0 + scale) + shift


def _mod_spec(arr, d):
    per_batch = arr.shape[0] > 1
    return pl.BlockSpec((1, 1, d), (lambda bi, i: (bi, 0, 0)) if per_batch else (lambda bi, i: (0, 0, 0)))


def _adaln_kernel(c_ref, w_ref, b_ref, o_ref):
    a = _silu(c_ref[...])
    w = w_ref[0]
    a1 = a.astype(BF16)
    a2 = (a - a1.astype(F32)).astype(BF16)
    w1 = w.astype(BF16)
    w2 = (w - w1.astype(F32)).astype(BF16)
    o_ref[0] = _dot(a1, w1) + _dot(a1, w2) + _dot(a2, w1) + b_ref[0]


def _adaln(cvecs, ada_w, ada_b):
    depth, d, n = ada_w.shape
    rows = cvecs.shape[0]
    tn = 1536
    return pl.pallas_call(
        _adaln_kernel,
        grid=(depth, n // tn),
        in_specs=[
            pl.BlockSpec((rows, d), lambda l, j: (0, 0)),
            pl.BlockSpec((1, d, tn), lambda l, j: (l, 0, j)),
            pl.BlockSpec((1, 1, tn), lambda l, j: (l, 0, j)),
        ],
        out_specs=pl.BlockSpec((1, rows, tn), lambda l, j: (l, 0, j)),
        out_shape=jax.ShapeDtypeStruct((depth, rows, n), F32),
        compiler_params=_params("parallel", "parallel"),
        name="adaln",
    )(cvecs, ada_w, ada_b.reshape(depth, 1, n))


def _proj_kernel(x_ref, nw_ref, sh_ref, sc_ref, w_ref, *rest, tn, rope_tiles):
    if rope_tiles:
        cos_ref, sa_ref, sb_ref, o_ref = rest
    else:
        (o_ref,) = rest
    h = _rms_mod(x_ref[0], nw_ref[...], sh_ref[0], sc_ref[0]).astype(BF16)
    n = w_ref.shape[1]
    for j in range(n // tn):
        cols = slice(j * tn, (j + 1) * tn)
        acc = _dot(h, w_ref[:, cols])
        if j in rope_tiles:
            tile = lambda r: jnp.concatenate([r[...]] * (tn // LANES), axis=1)
            acc = (acc * tile(cos_ref)
                   + pltpu.roll(acc, 16, 1) * tile(sa_ref)
                   + pltpu.roll(acc, tn - 16, 1) * tile(sb_ref))
        o_ref[0, :, cols] = acc.astype(o_ref.dtype)


def _proj(x, nw, shift, scale, w, out_dtype, tm, tn, rope=None, rope_tiles=()):
    b, t, d = x.shape
    n = w.shape[1]
    in_specs = [
        pl.BlockSpec((1, tm, d), lambda bi, i: (bi, i, 0)),
        pl.BlockSpec((1, d), lambda bi, i: (0, 0)),
        _mod_spec(shift, d), _mod_spec(scale, d),
        _resident((d, n)),
    ]
    args = [x, nw.reshape(1, d), shift, scale, w]
    if rope_tiles:
        in_specs += [pl.BlockSpec((tm, LANES), lambda bi, i: (i, 0))] * 3
        args += list(rope)
    return pl.pallas_call(
        functools.partial(_proj_kernel, tn=tn, rope_tiles=tuple(rope_tiles)),
        grid=(b, t // tm),
        in_specs=in_specs,
        out_specs=pl.BlockSpec((1, tm, n), lambda bi, i: (bi, i, 0)),
        out_shape=jax.ShapeDtypeStruct((b, t, n), out_dtype),
        compiler_params=_params("parallel", "parallel"),
        name="norm_proj",
    )(*args)


def _rope_tables(t):
    pos = jnp.arange(t)
    rows, cols = (pos // GRID_W).astype(F32), (pos % GRID_W).astype(F32)
    quarter = DIFF_HEAD_DIM // 4
    inv_freq = ROPE_THETA ** (-jnp.arange(quarter, dtype=F32) / quarter)
    lane = jnp.arange(LANES)
    within = lane % DIFF_HEAD_DIM
    use_row = within < DIFF_HEAD_DIM // 2
    freq = inv_freq[within % quarter]
    ang = jnp.where(use_row[None, :], rows[:, None], cols[:, None]) * freq[None, :]
    cos, sin = jnp.cos(ang), jnp.sin(ang)
    second = ((within // quarter) % 2) == 1
    sa = jnp.where(second[None, :], sin, 0.0)
    sb = jnp.where(second[None, :], 0.0, -sin)
    return cos, sa, sb


def _na_bias_table(rpb):
    kw = NA_WIN_COLS
    col = jnp.arange(GRID_W)
    col_start = jnp.clip(col - kw // 2, 0, GRID_W - kw)
    col_ok = (col[None, :] >= col_start[:, None]) & (col[None, :] < col_start[:, None] + kw)
    col_idx = jnp.clip(col[None, :] - col[:, None] + kw - 1, 0, 2 * kw - 2)
    rpb_cols = jnp.where(col_ok[None, None], rpb.astype(F32)[:, :, col_idx], MASK_VALUE)
    idx = jnp.arange(NA_WIN_ROWS)[:, None] + jnp.arange(NA_WIN_ROWS)[None, :]
    tb = rpb_cols[:, idx] * LOG2E
    nh = rpb.shape[0]
    tb = tb.reshape(nh // 2, 2, NA_WIN_ROWS, NA_WIN_ROWS, GRID_W, GRID_W).transpose(0, 2, 1, 4, 3, 5)
    return tb.reshape(nh // 2, NA_WIN_ROWS, 2 * GRID_W, NA_WIN_ROWS * GRID_W)


def _softmax_pv(s_list, v_list):
    m = functools.reduce(jnp.maximum, [s.max(axis=-1, keepdims=True) for s in s_list])
    es = [jnp.exp2(s - m) for s in s_list]
    l = functools.reduce(jnp.add, [e.sum(axis=-1, keepdims=True) for e in es])
    o = functools.reduce(jnp.add, [_dot(e.astype(BF16), v) for e, v in zip(es, v_list)])
    return o / l


def _stack_heads(q, lo):
    return jnp.concatenate([q * _mask01(lo), q * _mask01(jnp.logical_not(lo))], axis=0)


def _na_kernel(q_ref, k_ref, v_ref, kc_ref, vc_ref, tb_ref, o_ref, *, rows, rows_per_step):
    i = pl.program_id(2)
    lo = lax.broadcasted_iota(jnp.int32, (GRID_W, LANES), 1) < NA_HEAD_DIM
    kc, vc = kc_ref[0], vc_ref[0]
    win = NA_WIN_ROWS * GRID_W

    starts, scores = [], []
    for qa in range(rows_per_step):
        r = i * rows_per_step + qa
        r0 = jnp.clip(r - NA_WIN_ROWS // 2, 0, rows - NA_WIN_ROWS)
        off = r0 - r + NA_WIN_ROWS - 1
        qs = _stack_heads(q_ref[0, qa * GRID_W:(qa + 1) * GRID_W, :], lo)
        start = pl.multiple_of(r0 * GRID_W, GRID_W)
        starts.append(start)
        scores.append([_dot_nt(qs, k_ref[0, pl.ds(start, win), :]) + tb_ref[0, off], _dot_nt(qs, kc)])
    for qa in range(rows_per_step):
        o = _softmax_pv(scores[qa], [v_ref[0, pl.ds(starts[qa], win), :], vc])
        o = jnp.where(lo, o[:GRID_W], o[GRID_W:])
        o_ref[0, qa * GRID_W:(qa + 1) * GRID_W, :] = o.astype(o_ref.dtype)


def _na_attention(qkv, qkv_c, tb):
    b, t, _ = qkv.shape
    ctx = qkv_c.shape[1]
    rows = t // GRID_W
    rps = 8
    hp = NA_HEADS // 2
    tq = rps * GRID_W
    return pl.pallas_call(
        functools.partial(_na_kernel, rows=rows, rows_per_step=rps),
        grid=(b, hp, rows // rps),
        in_specs=[
            pl.BlockSpec((1, tq, LANES), lambda bi, h, i: (bi, i, h)),
            pl.BlockSpec((1, t, LANES), lambda bi, h, i: (bi, 0, hp + h)),
            pl.BlockSpec((1, t, LANES), lambda bi, h, i: (bi, 0, 2 * hp + h)),
            pl.BlockSpec((1, ctx, LANES), lambda bi, h, i: (bi, 0, hp + h)),
            pl.BlockSpec((1, ctx, LANES), lambda bi, h, i: (bi, 0, 2 * hp + h)),
            pl.BlockSpec((1, NA_WIN_ROWS, 2 * GRID_W, NA_WIN_ROWS * GRID_W), lambda bi, h, i: (h, 0, 0, 0)),
        ],
        out_specs=pl.BlockSpec((1, tq, LANES), lambda bi, h, i: (bi, i, h)),
        out_shape=jax.ShapeDtypeStruct((b, t, hp * LANES), BF16),
        compiler_params=_params("parallel", "parallel", "arbitrary"),
        name="na_attention",
    )(qkv, qkv, qkv, qkv_c, qkv_c, tb)


def _ctx_na_kernel(q_ref, k_ref, v_ref, o_ref):
    n = q_ref.shape[1]
    lo = lax.broadcasted_iota(jnp.int32, (n, LANES), 1) < NA_HEAD_DIM
    o = _softmax_pv([_dot_nt(_stack_heads(q_ref[0], lo), k_ref[0])], [v_ref[0]])
    o_ref[0] = jnp.where(lo, o[:n], o[n:]).astype(o_ref.dtype)


def _ctx_na_attention(qkv_c):
    b, ctx, _ = qkv_c.shape
    hp = NA_HEADS // 2
    return pl.pallas_call(
        _ctx_na_kernel,
        grid=(b, hp),
        in_specs=[
            pl.BlockSpec((1, ctx, LANES), lambda bi, h: (bi, 0, h)),
            pl.BlockSpec((1, ctx, LANES), lambda bi, h: (bi, 0, hp + h)),
            pl.BlockSpec((1, ctx, LANES), lambda bi, h: (bi, 0, 2 * hp + h)),
        ],
        out_specs=pl.BlockSpec((1, ctx, LANES), lambda bi, h: (bi, 0, h)),
        out_shape=jax.ShapeDtypeStruct((b, ctx, hp * LANES), BF16),
        compiler_params=_params("parallel", "parallel"),
        name="ctx_na_attention",
    )(qkv_c, qkv_c, qkv_c)


def _lam_full(lam_ref, lam_init):
    lv = lam_ref[...]
    s1 = jnp.sum(lv[0:1] * lv[1:2], axis=-1, keepdims=True)
    s2 = jnp.sum(lv[2:3] * lv[3:4], axis=-1, keepdims=True)
    return jnp.exp(s1) - jnp.exp(s2) + lam_init


def _diff_post(o, sub_ref, lam_init):
    y = o * lax.rsqrt(jnp.mean(o * o, axis=-1, keepdims=True) + EPS)
    return y * sub_ref[...] * (1.0 - lam_init)


def _diff_kernel(q_ref, kc_ref, vc_ref, *rest, tk, n_lat, lam_init, nstream):
    if n_lat:
        k_ref, v_ref, lam_ref, sub_ref, o_ref, vx_ref, sa_ref, sb_ref, vcx_ref, m_ref, acc_ref = rest
    else:
        lam_ref, sub_ref, o_ref, vcx_ref, m_ref, acc_ref = rest
    tq = q_ref.shape[1] // nstream
    lo = lax.broadcasted_iota(jnp.int32, (tq, LANES), 1) < DIFF_HEAD_DIM
    qs = jnp.concatenate([_stack_heads(q_ref[0, s * tq:(s + 1) * tq, :], lo) for s in range(nstream)], axis=0)
    nrow = qs.shape[0]
    halves = [slice(j * nrow // 2, (j + 1) * nrow // 2) for j in range(2)]

    @pl.when(pl.program_id(2) == 0)
    def _():
        vcx_ref[:, :LANES] = vc_ref[0]
        vcx_ref[:, LANES:] = jnp.ones((vcx_ref.shape[0], LANES), BF16)
        if n_lat:
            vx_ref[:, :LANES] = v_ref[0]
            vx_ref[:, LANES:] = jnp.ones((vx_ref.shape[0], LANES), BF16)

    def absorb(s, vxb, rows, first=False):
        m_old = m_ref[rows, :]
        m_new = s.max(axis=-1, keepdims=True) if first else jnp.maximum(m_old, s.max(axis=-1, keepdims=True))
        pv = _dot(jnp.exp2((s - m_new).astype(BF16)), vxb)
        acc_ref[rows, :] = pv if first else jnp.exp2(m_old - m_new) * acc_ref[rows, :] + pv
        m_ref[rows, :] = m_new

    s_ctx = _dot_nt(qs, kc_ref[0])
    if not n_lat:
        absorb(s_ctx, vcx_ref[...], slice(0, nrow), first=True)
    else:
        def lat_k(c):
            return k_ref[0, pl.ds(pl.multiple_of(c * tk, tk), tk), :]

        def lat_vx(c):
            return vx_ref[pl.ds(pl.multiple_of(c * tk, tk), tk), :]

        def phase(cur_ref, nxt_ref, c, prefetch):
            if prefetch:
                nxt_ref[...] = _dot_nt(qs, lat_k(c + 1))
            for rows in halves:
                absorb(cur_ref[rows, :], lat_vx(c), rows)

        sa_ref[...] = _dot_nt(qs, lat_k(0))
        absorb(s_ctx, vcx_ref[...], slice(0, nrow), first=True)

        def body(j, carry):
            phase(sa_ref, sb_ref, 2 * j, True)
            phase(sb_ref, sa_ref, 2 * j + 1, True)
            return carry

        lax.fori_loop(0, n_lat // 2 - 1, body, 0)
        phase(sa_ref, sb_ref, n_lat - 2, True)
        phase(sb_ref, sa_ref, n_lat - 1, False)
    lam_full = _lam_full(lam_ref, lam_init)
    for s in range(nstream):
        acc = acc_ref[2 * s * tq:2 * (s + 1) * tq, :]
        o = acc[:, :LANES] / acc[:, LANES:LANES + 1]
        o = o[:tq] - lam_full * o[tq:]
        o_ref[0, s * tq:(s + 1) * tq, :] = _diff_post(o, sub_ref, lam_init).astype(o_ref.dtype)


def _diff_attention(q_src, qkv_c, qkv, lam, subln, lam_init, tq, tk, nstream=1):
    b, tqs, _ = q_src.shape
    ctx = qkv_c.shape[1]
    h0 = 3 * NA_HEADS // 2
    nh = DIFF_HEADS
    in_specs = [
        pl.BlockSpec((1, tq, LANES), lambda bi, h, i: (bi, i, h0 + h)),
        pl.BlockSpec((1, ctx, LANES), lambda bi, h, i: (bi, 0, h0 + nh + h)),
        pl.BlockSpec((1, ctx, LANES), lambda bi, h, i: (bi, 0, h0 + 2 * nh + h)),
    ]
    args = [q_src, qkv_c, qkv_c]
    n_lat = 0
    scratch = []
    if qkv is not None:
        t = qkv.shape[1]
        n_lat = t // tk
        in_specs += [
            pl.BlockSpec((1, t, LANES), lambda bi, h, i: (bi, 0, h0 + nh + h)),
            pl.BlockSpec((1, t, LANES), lambda bi, h, i: (bi, 0, h0 + 2 * nh + h)),
        ]
        args += [qkv, qkv]
        assert n_lat % 2 == 0
        scratch += [pltpu.VMEM((t, 2 * LANES), BF16), pltpu.VMEM((2 * tq, tk), F32), pltpu.VMEM((2 * tq, tk), F32)]
    scratch += [pltpu.VMEM((ctx, 2 * LANES), BF16), pltpu.VMEM((2 * tq, 1), F32), pltpu.VMEM((2 * tq, 2 * LANES), F32)]
    in_specs += [
        pl.BlockSpec(lam.shape, lambda bi, h, i: (0, 0)),
        pl.BlockSpec((1, LANES), lambda bi, h, i: (0, 0)),
    ]
    args += [lam.astype(F32), subln.astype(F32).reshape(1, LANES)]
    return pl.pallas_call(
        functools.partial(_diff_kernel, tk=tk, n_lat=n_lat, lam_init=lam_init, nstream=nstream),
        grid=(b, nh, tqs // tq),
        in_specs=in_specs,
        out_specs=pl.BlockSpec((1, tq, LANES), lambda bi, h, i: (bi, i, h)),
        out_shape=jax.ShapeDtypeStruct((b, tqs, nh * LANES), BF16),
        scratch_shapes=scratch,
        compiler_params=_params("parallel", "parallel", "arbitrary"),
        name="diff_attention",
    )(*args)


def _chunk_masks(n):
    row = lax.broadcasted_iota(jnp.int32, (n, n), 0)
    col = lax.broadcasted_iota(jnp.int32, (n, n), 1)
    shift = int(math.log2(CHUNK))
    same = lax.shift_right_logical(row, shift) == lax.shift_right_logical(col, shift)
    return row, col, same


def _chunk_scan(x, reverse, pos):
    n, w = x.shape
    s = 1
    while s < 8:
        if reverse:
            x = x + jnp.where(pos < CHUNK - s, pltpu.roll(x, n - s, 0), 0.0)
        else:
            x = x + jnp.where(pos >= s, pltpu.roll(x, s, 0), 0.0)
        s *= 2
    while s < CHUNK:
        parts = []
        for c in range(n // CHUNK):
            xc = x[c * CHUNK:(c + 1) * CHUNK]
            pad = jnp.zeros((s, w), F32)
            parts.append(xc + (jnp.concatenate([xc[s:], pad], axis=0) if reverse
                               else jnp.concatenate([pad, xc[:CHUNK - s]], axis=0)))
        x = jnp.concatenate(parts, axis=0)
        s *= 2
    return x


def _chunk_totals(G, reverse):
    n, w = G.shape
    r8 = lax.broadcasted_iota(jnp.int32, (8, w), 0)
    parts, dec = [], jnp.zeros((8, w), F32)
    for c in range(n // CHUNK):
        r = c * CHUNK + (0 if reverse else CHUNK - 1)
        tot = G[r:r + 1, :]
        parts.append(jnp.broadcast_to(tot, (CHUNK, w)))
        dec = jnp.where(r8 == c, jnp.broadcast_to(tot, (8, w)), dec)
    return jnp.concatenate(parts, axis=0), dec


def _rec_prep_kernel(dq_ref, dk_ref, dv_ref, pq_ref, pk_ref, pv_ref, nq_ref, nk_ref, nv_ref,
                     hq_ref, hi_ref, hff_ref, hfb_ref, dab_ref, cwq_ref, cwk_ref, cwv_ref, lb_ref, ab_ref,
                     cqe_ref, ckt_ref, coi_ref, cdec_ref, cv_ref,
                     dw_ref, du_ref, dkt_ref, dqe_ref, daqk_ref, ddec_ref, *, nblk):
    i = pl.program_id(1)
    n = REC_BLOCK
    w = REC_W
    row, col, same = _chunk_masks(PAIR)
    incl = (same & (row >= col), same & (row <= col))
    strict = (same & (row > col), same & (row < col))
    rid = lax.broadcasted_iota(jnp.int32, (n, w), 0)
    pos = rid & (CHUNK - 1)
    pos1 = pos[:, :LANES]
    pairs = [slice(p * PAIR, (p + 1) * PAIR) for p in range(n // PAIR)]
    heads = [slice(h * LANES, (h + 1) * LANES) for h in range(REC_HEADS)]
    keys = [(h, d, p) for h in range(REC_HEADS) for d in range(2) for p in range(len(pairs))]
    hp_keys = [(h, p) for h in range(REC_HEADS) for p in range(len(pairs))]

    cq = _silu(hq_ref[0].astype(F32)) * (REC_HEAD_DIM ** -0.5)
    cvb = hi_ref[0]
    cv_ref[0] = cvb
    qin, ktail = {}, {}
    for d, f_ref in enumerate((hff_ref, hfb_ref)):
        lbv = lb_ref[pl.ds(d, 1), :]
        sig = _sigmoid(f_ref[0])
        g = jnp.log(lbv + (1.0 - lbv) * sig)
        ck = (1.0 - lbv) * (1.0 - sig)
        G = _chunk_scan(g, d == 1, pos)
        Gl, dec = _chunk_totals(G, d == 1)
        ktail[d] = (ck * jnp.exp(Gl - G)).astype(BF16)
        qin[d] = (cq * jnp.exp(G - Gl)).astype(BF16)
        cqe_ref[d, 0] = (cq * jnp.exp(G)).astype(BF16)
        ckt_ref[d, 0] = ktail[d]
        cdec_ref[d, 0, 0] = jnp.exp(dec)
    amat = {(h, d, p): jnp.where(incl[d], _dot_nt(qin[d][pairs[p], heads[h]], ktail[d][pairs[p], heads[h]]), 0.0)
            for h, d, p in keys}
    for h, d, p in keys:
        coi_ref[d, 0, pairs[p], heads[h]] = _dot(amat[h, d, p].astype(BF16), cvb[pairs[p], heads[h]])

    last = BF16_ROWS - 1

    def conv_silu(x_ref, p_ref, n_ref, cw_ref):
        x = x_ref[0].astype(F32)
        cw = cw_ref[...]
        prev_row = jnp.where(i > 0, p_ref[0, last:last + 1, :].astype(F32), 0.0)
        next_row = jnp.where(i < nblk - 1, n_ref[0, 0:1, :].astype(F32), 0.0)
        xp = jnp.where(rid == 0, prev_row, pltpu.roll(x, 1, 0))
        xn = jnp.where(rid == n - 1, next_row, pltpu.roll(x, n - 1, 0))
        return _silu(cw[0:1] * xp + cw[1:2] * x + cw[2:3] * xn)

    def l2n(x):
        return jnp.concatenate(
            [x[:, hs] * lax.rsqrt(jnp.sum(x[:, hs] * x[:, hs], axis=-1, keepdims=True) + EPS) for hs in heads], axis=1)

    qn = l2n(conv_silu(dq_ref, pq_ref, nq_ref, cwq_ref)) * (REC_HEAD_DIM ** -0.5)
    kn = l2n(conv_silu(dk_ref, pk_ref, nk_ref, cwk_ref))
    vv = conv_silu(dv_ref, pv_ref, nv_ref, cwv_ref)
    knb, qnb = kn.astype(BF16), qn.astype(BF16)
    kk = {(h, p): _dot_nt(knb[pairs[p], heads[h]], knb[pairs[p], heads[h]]) for h, p in hp_keys}
    qk = {(h, p): _dot_nt(qnb[pairs[p], heads[h]], knb[pairs[p], heads[h]]) for h, p in hp_keys}
    tile = dab_ref[0]
    arate, dtb = ab_ref[0:1, :], ab_ref[1:2, :]
    xs = tile + dtb
    gt = -arate * (jnp.maximum(xs, 0.0) + jnp.log(1.0 + jnp.exp(-jnp.abs(xs))))
    bt = _sigmoid(tile)
    lane = lax.broadcasted_iota(jnp.int32, (n, LANES), 1)
    spread = lambda t, j0: jnp.concatenate(
        [jnp.broadcast_to(jnp.sum(jnp.where(lane == j0 + h, t, 0.0), axis=-1, keepdims=True), (n, LANES))
         for h in range(REC_HEADS)], axis=1)
    eye = (row == col).astype(F32)
    rhs, gmat = {}, {}
    for d in range(2):
        gsc = _chunk_scan(gt, d == 1, pos1)
        Gc = spread(gsc, d * REC_HEADS)
        bb = spread(bt, (2 + d) * REC_HEADS)
        Gl, dec = _chunk_totals(Gc, d == 1)
        eG = jnp.exp(Gc)
        rhs[d] = ((kn * bb * eG).astype(BF16), (vv * bb).astype(BF16))
        for h, p in hp_keys:
            gcp = Gc[pairs[p], heads[h]]
            gamma = jnp.exp(jnp.where(incl[d], gcp - gcp.T, MASK_VALUE))
            gmat[h, d, p] = jnp.where(strict[d], bb[pairs[p], heads[h]] * kk[h, p] * gamma, 0.0)
            daqk_ref[d, 0, pairs[p], heads[h]] = jnp.where(incl[d], qk[h, p] * gamma, 0.0).astype(BF16)
        dkt_ref[d, 0] = (kn * jnp.exp(Gl - Gc)).astype(BF16)
        dqe_ref[d, 0] = (qn * eG).astype(BF16)
        ddec_ref[d, 0, 0] = jnp.exp(dec)

    T = {key: eye for key in keys}
    for k in range(int(math.log2(CHUNK))):
        blk = lax.shift_right_logical(row, k + 1) == lax.shift_right_logical(col, k + 1)
        sub = lax.shift_right_logical(row, k) != lax.shift_right_logical(col, k)
        ak = {key: jnp.where(blk & sub, gmat[key], 0.0) for key in keys}
        if k == 0:
            T = {key: T[key] - ak[key] for key in keys}
            continue
        Tb = {key: T[key].astype(BF16) for key in keys}
        x = {key: _dot(ak[key].astype(BF16), Tb[key]).astype(BF16) for key in keys}
        T = {key: T[key] - _dot(Tb[key], x[key]) for key in keys}
    for h, d, p in keys:
        both = jnp.concatenate([rhs[d][0][pairs[p], heads[h]], rhs[d][1][pairs[p], heads[h]]], axis=1)
        wu = _dot(T[h, d, p].astype(BF16), both)
        dw_ref[d, 0, pairs[p], heads[h]] = wu[:, :LANES].astype(BF16)
        du_ref[d, 0, pairs[p], heads[h]] = wu[:, LANES:].astype(BF16)


def _rec_prep(p16, p32, conv_w, lb, ab):
    b, t, _ = p16.shape
    nblk = t // REC_BLOCK
    n = REC_BLOCK
    w = REC_W
    per = n // BF16_ROWS
    main = lambda blk: pl.BlockSpec((1, n, w), lambda bi, i: (bi, i, blk))
    prev = lambda blk: pl.BlockSpec((1, BF16_ROWS, w), lambda bi, i: (bi, jnp.maximum(i * per - 1, 0), blk))
    nxt = lambda blk: pl.BlockSpec((1, BF16_ROWS, w),
                                   lambda bi, i: (bi, jnp.minimum((i + 1) * per, t // BF16_ROWS - 1), blk))
    taps = lambda blk: pl.BlockSpec((8, w), lambda bi, i: (0, blk))
    in_specs = [main(P16_DQ), main(P16_DK), main(P16_DV), prev(P16_DQ), prev(P16_DK), prev(P16_DV),
                nxt(P16_DQ), nxt(P16_DK), nxt(P16_DV), main(P16_HQ), main(P16_HI),
                main(P32_HFF), main(P32_HFB),
                pl.BlockSpec((1, n, LANES), lambda bi, i: (bi, i, P32_DAB_TILE)),
                taps(0), taps(1), taps(2),
                pl.BlockSpec((8, w), lambda bi, i: (0, 0)),
                pl.BlockSpec((8, LANES), lambda bi, i: (0, 0))]
    dir_tok = lambda dt: (jax.ShapeDtypeStruct((2, b, t, w), dt),
                          pl.BlockSpec((2, 1, n, w), lambda bi, i: (0, bi, i, 0)))
    dir_dec = (jax.ShapeDtypeStruct((2, b, nblk, 8, w), F32),
               pl.BlockSpec((2, 1, 1, 8, w), lambda bi, i: (0, bi, i, 0, 0)))
    outs = [dir_tok(BF16), dir_tok(BF16), dir_tok(F32), dir_dec,
            (jax.ShapeDtypeStruct((b, t, w), BF16), pl.BlockSpec((1, n, w), lambda bi, i: (bi, i, 0))),
            dir_tok(BF16), dir_tok(BF16), dir_tok(BF16), dir_tok(BF16), dir_tok(BF16), dir_dec]
    return pl.pallas_call(
        functools.partial(_rec_prep_kernel, nblk=nblk),
        grid=(b, nblk),
        in_specs=in_specs,
        out_specs=[o[1] for o in outs],
        out_shape=[o[0] for o in outs],
        compiler_params=_params("parallel", "arbitrary"),
        name="rec_prep",
    )(*([p16] * 11), p32, p32, p32, conv_w, conv_w, conv_w, lb, ab)


_N_PREP = 11


def _rec_scan_kernel(*refs, nblk, has_init, with_output):
    ins = [refs[d * _N_PREP:(d + 1) * _N_PREP] for d in range(2)]
    rest = list(refs[2 * _N_PREP:])
    s0c_ref, s0d_ref = (rest.pop(0), rest.pop(0)) if has_init else (None, None)
    if with_output:
        oc_refs, od_refs = (rest.pop(0), rest.pop(0)), (rest.pop(0), rest.pop(0))
    sfc_ref, sfd_ref, sc_ref, sd_ref = rest
    i = pl.program_id(1)

    @pl.when(i == 0)
    def _():
        if has_init:
            sc_ref[...] = s0c_ref[0]
            sd_ref[...] = s0d_ref[0]
        else:
            sc_ref[...] = jnp.zeros_like(sc_ref)
            sd_ref[...] = jnp.zeros_like(sd_ref)

    nchunk = REC_BLOCK // CHUNK
    keys = [(d, h) for d in range(2) for h in range(REC_HEADS)]
    for step in range(nchunk):
        chunk_of = lambda d: step if d == 0 else nchunk - 1 - step

        def tok(d, idx, h):
            c = chunk_of(d)
            ref = ins[d][idx]
            sl = (slice(c * CHUNK, (c + 1) * CHUNK), slice(h * LANES, (h + 1) * LANES))
            return ref[(0,) * (len(ref.shape) - 2) + sl]

        def dec(d, idx, h):
            c = chunk_of(d)
            return ins[d][idx][0, 0, 0, c:c + 1, h * LANES:(h + 1) * LANES]

        def out_slice(d, h):
            c = chunk_of(d)
            return (0, slice(c * CHUNK, (c + 1) * CHUNK), slice(h * LANES, (h + 1) * LANES))

        S = {k: sd_ref[k[0], k[1]] for k in keys}
        Sb = {k: S[k].astype(BF16) for k in keys}
        if with_output:
            both = {(d, h): _dot(jnp.concatenate([tok(d, 5, h), tok(d, 8, h)], axis=0), Sb[d, h]) for d, h in keys}
            ws = {k: both[k][:CHUNK] for k in keys}
            qs = {k: both[k][CHUNK:] for k in keys}
        else:
            ws = {(d, h): _dot(tok(d, 5, h), Sb[d, h]) for d, h in keys}
        ST = {k: sc_ref[k[0], k[1]] for k in keys}
        if with_output:
            for d, h in keys:
                oc_refs[d][out_slice(d, h)] = (
                    tok(d, 2, h) + _dot_nt(tok(d, 0, h), ST[d, h].astype(BF16))).astype(BF16)
        for d, h in keys:
            sc_ref[d, h] = ST[d, h] * dec(d, 3, h) + _dot_tn(tok(d, 4, h), tok(d, 1, h))
        vnb = {(d, h): (tok(d, 6, h).astype(F32) - ws[d, h]).astype(BF16) for d, h in keys}
        for d, h in keys:
            sd_ref[d, h] = S[d, h] * dec(d, 10, h) + _dot_tn(tok(d, 7, h), vnb[d, h])
        if with_output:
            for d, h in keys:
                vpair = jnp.concatenate([vnb[d, h], vnb[d, h]], axis=0)
                od_refs[d][out_slice(d, h)] = (qs[d, h] + _dot(tok(d, 9, h), vpair)).astype(BF16)

    @pl.when(i == nblk - 1)
    def _():
        sfc_ref[0] = sc_ref[...]
        sfd_ref[0] = sd_ref[...]


def _rec_scan(prep, init, with_output):
    _, b, t, w = prep[0].shape
    nblk = t // REC_BLOCK
    n = REC_BLOCK
    in_specs, args = [], []
    for d in range(2):
        blk = (lambda i: i) if d == 0 else (lambda i: nblk - 1 - i)
        tok = pl.BlockSpec((1, 1, n, w), lambda bi, i, d=d, blk=blk: (d, bi, blk(i), 0))
        dec = pl.BlockSpec((1, 1, 1, 8, w), lambda bi, i, d=d, blk=blk: (d, bi, blk(i), 0, 0))
        cvs = pl.BlockSpec((1, n, w), lambda bi, i, blk=blk: (bi, blk(i), 0))
        in_specs += [tok, tok, tok, dec, cvs, tok, tok, tok, tok, tok, dec]
        args += list(prep)
    state = pl.BlockSpec((1, 2, REC_HEADS, LANES, LANES), lambda bi, i: (bi, 0, 0, 0, 0))
    if init is not None:
        in_specs += [state, state]
        args += list(init)
    out_specs, out_shape = [], []
    if with_output:
        fwd = pl.BlockSpec((1, n, w), lambda bi, i: (bi, i, 0))
        bwd = pl.BlockSpec((1, n, w), lambda bi, i: (bi, nblk - 1 - i, 0))
        out_specs += [fwd, bwd, fwd, bwd]
        out_shape += [jax.ShapeDtypeStruct((b, t, w), BF16)] * 4
    out_specs += [state, state]
    out_shape += [jax.ShapeDtypeStruct((b, 2, REC_HEADS, LANES, LANES), F32)] * 2
    res = pl.pallas_call(
        functools.partial(_rec_scan_kernel, nblk=nblk, has_init=init is not None, with_output=with_output),
        grid=(b, nblk),
        in_specs=in_specs,
        out_specs=out_specs,
        out_shape=out_shape,
        scratch_shapes=[pltpu.VMEM((2, REC_HEADS, LANES, LANES), F32)] * 2,
        compiler_params=_params("parallel", "arbitrary"),
        name="rec_scan",
    )(*args)
    return (res[:4], (res[4], res[5])) if with_output else (None, (res[0], res[1]))


def _gated_head_norm(f_ref, b_ref, gate_ref, nw_ref):
    o = f_ref[0].astype(F32) + b_ref[0].astype(F32)
    gate = _silu(gate_ref[0].astype(F32))
    parts = []
    for h in range(REC_HEADS):
        oh = o[:, h * LANES:(h + 1) * LANES]
        parts.append(oh * lax.rsqrt(jnp.mean(oh * oh, axis=-1, keepdims=True) + EPS) * nw_ref[...])
    return (jnp.concatenate(parts, axis=1) * gate).astype(BF16)


def _tail_kernel(*refs, recurrent, final, f_splits):
    refs = list(refs)
    x_ref = refs.pop(0)
    if recurrent:
        ocf, ocb, odf, odb, cg, dg, cn, dn = [refs.pop(0) for _ in range(8)]
        ya, yb = _gated_head_norm(ocf, ocb, cg, cn), _gated_head_norm(odf, odb, dg, dn)
    else:
        ya, yb = refs.pop(0)[0], refs.pop(0)[0]
    wa_ref, wb_ref, gm_ref, nw_ref, sh_ref, sc_ref, gf_ref, wg_ref, wu_ref, wd_ref = refs[:10]
    o_ref = refs[-1]
    x1 = x_ref[0] + gm_ref[0] * (_dot(ya, wa_ref[...]) + _dot(yb, wb_ref[...]))
    h = _rms_mod(x1, nw_ref[...], sh_ref[0], sc_ref[0]).astype(BF16)
    acc = None
    for f0, f1 in f_splits:
        z = (_silu(_dot(h, wg_ref[:, f0:f1])) * _dot(h, wu_ref[:, f0:f1])).astype(BF16)
        part = _dot(z, wd_ref[f0:f1, :])
        acc = part if acc is None else acc + part
    out = x1 + gf_ref[0] * acc
    if final:
        out = out * lax.rsqrt(jnp.mean(out * out, axis=-1, keepdims=True) + EPS) * refs[10][...]
    o_ref[0] = out


def _tail(x, mix_inputs, wa, wb, g_mix, nw, shift, scale, g_ffn, wg, wu, wd, final_norm, tm, norms=None):
    b, t, d = x.shape
    f = wg.shape[1]
    recurrent = norms is not None
    tokw = lambda width, blk=0: pl.BlockSpec((1, tm, width), lambda bi, i: (bi, i, blk))
    vec = lambda width: pl.BlockSpec((1, width), lambda bi, i: (0, 0))
    in_specs = [tokw(d)]
    args = [x]
    if recurrent:
        *outs, p16 = mix_inputs
        in_specs += [tokw(REC_W)] * 4 + [tokw(REC_W, P16_HGATE), tokw(REC_W, P16_DGATE), vec(LANES), vec(LANES)]
        args += list(outs) + [p16, p16, norms[0].reshape(1, LANES), norms[1].reshape(1, LANES)]
    else:
        in_specs += [tokw(a.shape[2]) for a in mix_inputs]
        args += list(mix_inputs)
    in_specs += [_resident(wa.shape), _resident(wb.shape), _mod_spec(g_mix, d), vec(d),
                 _mod_spec(shift, d), _mod_spec(scale, d), _mod_spec(g_ffn, d),
                 _resident((d, f)), _resident((d, f)), _resident((f, d))]
    args += [wa, wb, g_mix, nw.reshape(1, d), shift, scale, g_ffn, wg, wu, wd]
    if final_norm is not None:
        in_specs.append(vec(d))
        args.append(final_norm.reshape(1, d))
    mxu = 2 * LANES
    f_mid = -(-f // (2 * mxu)) * mxu
    return pl.pallas_call(
        functools.partial(_tail_kernel, recurrent=recurrent, final=final_norm is not None,
                          f_splits=((0, f_mid), (f_mid, f))),
        grid=(b, t // tm),
        in_specs=in_specs,
        out_specs=pl.BlockSpec((1, tm, d), lambda bi, i: (bi, i, 0)),
        out_shape=jax.ShapeDtypeStruct((b, t, d), F32),
        compiler_params=_params("parallel", "parallel"),
        name="layer_tail",
    )(*args)


def _pad_rows(a, rows):
    return jnp.pad(a, ((0, rows - a.shape[0]),) + ((0, 0),) * (a.ndim - 1))


def kernel(x, c, ctx, c_ctx, ada_w, ada_b, norm_mix, norm_ffn, w_mix_out, ffn_gate, ffn_up, ffn_down,
           att_w_in, att_rpb, att_lambda, att_subln, rec_w_in, rec_lb_logits, rec_conv_w, rec_a_log,
           rec_dt_bias, rec_c_norm, rec_d_norm, final_norm):
    b, t, d = x.shape
    n_ctx = ctx.shape[1]
    depth = ada_w.shape[0]
    assert depth == 2 and t % (8 * GRID_W) == 0 and n_ctx % REC_BLOCK == 0
    tm = 512 if t % 512 == 0 else 256
    tmc = 256

    nrow = -(-(b + 1) // 8) * 8
    mods = _adaln(_pad_rows(jnp.concatenate([c, c_ctx[None, :]], axis=0).astype(F32), nrow), ada_w, ada_b)

    def layer_mods(layer):
        m = mods[layer].reshape(nrow, 6, 1, d)
        return [m[:b, k] for k in range(6)], [m[b:b + 1, k] for k in range(6)]

    bf = lambda a: a.astype(BF16)
    h_ctx = ctx

    ml, mc = layer_mods(0)
    na_w, df_w = NA_HEADS * NA_HEAD_DIM, 2 * DIFF_HEADS * DIFF_HEAD_DIM
    qscale = jnp.ones((att_w_in.shape[2],), F32)
    qscale = qscale.at[:na_w].set(NA_HEAD_DIM ** -0.5 * LOG2E).at[3 * na_w:3 * na_w + df_w].set(DIFF_HEAD_DIM ** -0.5 * LOG2E)
    w_in = bf(att_w_in[0] * qscale[None, :])
    rope = _rope_tables(t)
    qkv = _proj(x, norm_mix[0], ml[0], ml[1], w_in, BF16, tm, 512, rope=rope, rope_tiles=(3, 4))
    qkv_c = _proj(h_ctx, norm_mix[0], mc[0], mc[1], w_in, BF16, tmc, 512)
    lam_init = 0.8 - 0.6 * math.exp(-0.3 * 0)
    tb = _na_bias_table(att_rpb[0])
    y_na = _na_attention(qkv, qkv_c, tb)
    y_df = _diff_attention(qkv, qkv_c, qkv, att_lambda[0], att_subln[0], lam_init, tq=1024, tk=min(1024, t // 2), nstream=4)
    yc_na = _ctx_na_attention(qkv_c)
    yc_df = _diff_attention(qkv_c, qkv_c, None, att_lambda[0], att_subln[0], lam_init, tq=n_ctx, tk=0)
    wo = bf(w_mix_out[0])
    wg, wu, wd = bf(ffn_gate[0]), bf(ffn_up[0]), bf(ffn_down[0])
    x = _tail(x, (y_na, y_df), wo[:na_w], wo[na_w:], ml[2], norm_ffn[0], ml[3], ml[4], ml[5], wg, wu, wd, None, tm)
    h_ctx = _tail(h_ctx, (yc_na, yc_df), wo[:na_w], wo[na_w:], mc[2], norm_ffn[0], mc[3], mc[4], mc[5], wg, wu, wd,
                  None, tmc)

    ml, mc = layer_mods(1)
    hw = REC_W
    wr = rec_w_in[0]
    cuts = [0, hw, 2 * hw, 3 * hw, 4 * hw, 5 * hw, 8 * hw, 8 * hw + 4 * REC_HEADS, 9 * hw + 4 * REC_HEADS]
    hq, hff, hfb, hi, hgate, dqkv, dab, dgate = [wr[:, cuts[k]:cuts[k + 1]] for k in range(8)]
    w16 = bf(jnp.concatenate([dqkv, hq, hi, hgate, dgate], axis=1))
    w32 = bf(jnp.concatenate([hff, hfb, jnp.pad(dab, ((0, 0), (0, LANES - dab.shape[1])))], axis=1))
    lbs = jax.nn.softmax(rec_lb_logits.astype(F32), axis=0)
    lb = _pad_rows((jnp.cumsum(lbs, axis=0) - lbs[0])[1], 8)
    conv_w = _pad_rows(rec_conv_w[0].astype(F32), 8)
    a_rate = jnp.exp(rec_a_log[0].astype(F32)).reshape(-1)
    dtb = rec_dt_bias[0].astype(F32).reshape(-1)
    ab = jnp.zeros((8, LANES), F32).at[0, :a_rate.shape[0]].set(a_rate).at[1, :dtb.shape[0]].set(dtb)

    def rec_inputs(h, mods_, tile):
        p16 = _proj(h, norm_mix[1], mods_[0], mods_[1], w16, BF16, tile, 512)
        p32 = _proj(h, norm_mix[1], mods_[0], mods_[1], w32, F32, tile, w32.shape[1])
        return p16, _rec_prep(p16, p32, conv_w, lb, ab)

    _, prep_c = rec_inputs(h_ctx, mc, tmc)
    p16, prep_l = rec_inputs(x, ml, tm)
    _, s_ctx = _rec_scan(prep_c, None, with_output=False)
    outs, _ = _rec_scan(prep_l, s_ctx, with_output=True)
    wo = bf(w_mix_out[1])
    return _tail(x, (*outs, p16), wo[:hw], wo[hw:], ml[2], norm_ffn[1], ml[3], ml[4], ml[5],
                 bf(ffn_gate[1]), bf(ffn_up[1]), bf(ffn_down[1]), final_norm, tm,
                 norms=(rec_c_norm[0].astype(F32), rec_d_norm[0].astype(F32)))
```

```python
import functools
import math

import jax
import jax.numpy as jnp
from jax import lax
from jax.experimental import pallas as pl
from jax.experimental.pallas import tpu as pltpu

F32 = jnp.float32
BF16 = jnp.bfloat16

EPS = 1e-6
MASK_VALUE = -1e30
GRID_W = 64
NA_HEADS = 8
NA_HEAD_DIM = 64
NA_WIN_ROWS = 8
NA_WIN_COLS = 16
DIFF_HEADS = 4
DIFF_HEAD_DIM = 64
REC_HEADS = 4
REC_HEAD_DIM = 128
CHUNK = 64
ROPE_THETA = 10000.0
LOG2E = math.log2(math.e)
LANES = 128
BF16_ROWS = 16
PAIR = 2 * CHUNK
REC_BLOCK = 4 * CHUNK
REC_W = REC_HEADS * REC_HEAD_DIM
VMEM_LIMIT = 56 * 1024 * 1024

P16_DQ, P16_DK, P16_DV, P16_HQ, P16_HI, P16_HGATE, P16_DGATE = range(7)
P32_HFF, P32_HFB = 0, 1
P32_DAB_TILE = 2 * REC_HEADS

_NT = (((1,), (1,)), ((), ()))
_TN = (((0,), (0,)), ((), ()))


def _params(*sem):
    return pltpu.CompilerParams(dimension_semantics=sem, vmem_limit_bytes=VMEM_LIMIT)


def _resident(shape):
    return pl.BlockSpec(shape, lambda *_: (0,) * len(shape), pipeline_mode=pl.Buffered(1))


def _dot(a, b):
    return jnp.dot(a, b, preferred_element_type=F32)


def _dot_nt(a, b):
    return lax.dot_general(a, b, _NT, preferred_element_type=F32)


def _dot_tn(a, b):
    return lax.dot_general(a, b, _TN, preferred_element_type=F32)


def _mask01(m):
    return jnp.where(m, 1.0, 0.0).astype(BF16)


def _sigmoid(x):
    return 0.5 * jnp.tanh(0.5 * x) + 0.5


def _silu(x):
    return x * _sigmoid(x)


def _rms_mod(x, nw, shift, scale):
    y = x * lax.rsqrt(jnp.mean(x * x, axis=-1, keepdims=True) + EPS)
    return (y * nw) * (1.0 + scale) + shift


def _mod_spec(arr, d):
    per_batch = arr.shape[0] > 1
    return pl.BlockSpec((1, 1, d), (lambda bi, i: (bi, 0, 0)) if per_batch else (lambda bi, i: (0, 0, 0)))


def _adaln_kernel(c_ref, w_ref, b_ref, o_ref):
    a = _silu(c_ref[...])
    w = w_ref[0]
    a1 = a.astype(BF16)
    a2 = (a - a1.astype(F32)).astype(BF16)
    w1 = w.astype(BF16)
    w2 = (w - w1.astype(F32)).astype(BF16)
    o_ref[0] = _dot(a1, w1) + _dot(a1, w2) + _dot(a2, w1) + b_ref[0]


def _adaln(cvecs, ada_w, ada_b):
    depth, d, n = ada_w.shape
    rows = cvecs.shape[0]
    tn = 1536
    return pl.pallas_call(
        _adaln_kernel,
        grid=(depth, n // tn),
        in_specs=[
            pl.BlockSpec((rows, d), lambda l, j: (0, 0)),
            pl.BlockSpec((1, d, tn), lambda l, j: (l, 0, j)),
            pl.BlockSpec((1, 1, tn), lambda l, j: (l, 0, j)),
        ],
        out_specs=pl.BlockSpec((1, rows, tn), lambda l, j: (l, 0, j)),
        out_shape=jax.ShapeDtypeStruct((depth, rows, n), F32),
        compiler_params=_params("parallel", "parallel"),
        name="adaln",
    )(cvecs, ada_w, ada_b.reshape(depth, 1, n))


def _proj_kernel(x_ref, nw_ref, sh_ref, sc_ref, w_ref, *rest, tn, rope_tiles):
    if rope_tiles:
        cos_ref, sa_ref, sb_ref, o_ref = rest
    else:
        (o_ref,) = rest
    h = _rms_mod(x_ref[0], nw_ref[...], sh_ref[0], sc_ref[0]).astype(BF16)
    n = w_ref.shape[1]
    for j in range(n // tn):
        cols = slice(j * tn, (j + 1) * tn)
        acc = _dot(h, w_ref[:, cols])
        if j in rope_tiles:
            tile = lambda r: jnp.concatenate([r[...]] * (tn // LANES), axis=1)
            acc = (acc * tile(cos_ref)
                   + pltpu.roll(acc, 16, 1) * tile(sa_ref)
                   + pltpu.roll(acc, tn - 16, 1) * tile(sb_ref))
        o_ref[0, :, cols] = acc.astype(o_ref.dtype)


def _proj(x, nw, shift, scale, w, out_dtype, tm, tn, rope=None, rope_tiles=()):
    b, t, d = x.shape
    n = w.shape[1]
    in_specs = [
        pl.BlockSpec((1, tm, d), lambda bi, i: (bi, i, 0)),
        pl.BlockSpec((1, d), lambda bi, i: (0, 0)),
        _mod_spec(shift, d), _mod_spec(scale, d),
        _resident((d, n)),
    ]
    args = [x, nw.reshape(1, d), shift, scale, w]
    if rope_tiles:
        in_specs += [pl.BlockSpec((tm, LANES), lambda bi, i: (i, 0))] * 3
        args += list(rope)
    return pl.pallas_call(
        functools.partial(_proj_kernel, tn=tn, rope_tiles=tuple(rope_tiles)),
        grid=(b, t // tm),
        in_specs=in_specs,
        out_specs=pl.BlockSpec((1, tm, n), lambda bi, i: (bi, i, 0)),
        out_shape=jax.ShapeDtypeStruct((b, t, n), out_dtype),
        compiler_params=_params("parallel", "parallel"),
        name="norm_proj",
    )(*args)


def _rope_tables(t):
    pos = jnp.arange(t)
    rows, cols = (pos // GRID_W).astype(F32), (pos % GRID_W).astype(F32)
    quarter = DIFF_HEAD_DIM // 4
    inv_freq = ROPE_THETA ** (-jnp.arange(quarter, dtype=F32) / quarter)
    lane = jnp.arange(LANES)
    within = lane % DIFF_HEAD_DIM
    use_row = within < DIFF_HEAD_DIM // 2
    freq = inv_freq[within % quarter]
    ang = jnp.where(use_row[None, :], rows[:, None], cols[:, None]) * freq[None, :]
    cos, sin = jnp.cos(ang), jnp.sin(ang)
    second = ((within // quarter) % 2) == 1
    sa = jnp.where(second[None, :], sin, 0.0)
    sb = jnp.where(second[None, :], 0.0, -sin)
    return cos, sa, sb


def _na_bias_table(rpb):
    kw = NA_WIN_COLS
    col = jnp.arange(GRID_W)
    col_start = jnp.clip(col - kw // 2, 0, GRID_W - kw)
    col_ok = (col[None, :] >= col_start[:, None]) & (col[None, :] < col_start[:, None] + kw)
    col_idx = jnp.clip(col[None, :] - col[:, None] + kw - 1, 0, 2 * kw - 2)
    rpb_cols = jnp.where(col_ok[None, None], rpb.astype(F32)[:, :, col_idx], MASK_VALUE)
    idx = jnp.arange(NA_WIN_ROWS)[:, None] + jnp.arange(NA_WIN_ROWS)[None, :]
    tb = rpb_cols[:, idx] * LOG2E
    nh = rpb.shape[0]
    tb = tb.reshape(nh // 2, 2, NA_WIN_ROWS, NA_WIN_ROWS, GRID_W, GRID_W).transpose(0, 2, 1, 4, 3, 5)
    return tb.reshape(nh // 2, NA_WIN_ROWS, 2 * GRID_W, NA_WIN_ROWS * GRID_W)


def _softmax_pv(s_list, v_list):
    m = functools.reduce(jnp.maximum, [s.max(axis=-1, keepdims=True) for s in s_list])
    es = [jnp.exp2(s - m) for s in s_list]
    l = functools.reduce(jnp.add, [e.sum(axis=-1, keepdims=True) for e in es])
    o = functools.reduce(jnp.add, [_dot(e.astype(BF16), v) for e, v in zip(es, v_list)])
    return o / l


def _stack_heads(q, lo):
    return jnp.concatenate([q * _mask01(lo), q * _mask01(jnp.logical_not(lo))], axis=0)


def _na_kernel(q_ref, k_ref, v_ref, kc_ref, vc_ref, tb_ref, o_ref, *, rows, rows_per_step):
    i = pl.program_id(2)
    lo = lax.broadcasted_iota(jnp.int32, (GRID_W, LANES), 1) < NA_HEAD_DIM
    kc, vc = kc_ref[0], vc_ref[0]
    win = NA_WIN_ROWS * GRID_W

    starts, scores = [], []
    for qa in range(rows_per_step):
        r = i * rows_per_step + qa
        r0 = jnp.clip(r - NA_WIN_ROWS // 2, 0, rows - NA_WIN_ROWS)
        off = r0 - r + NA_WIN_ROWS - 1
        qs = _stack_heads(q_ref[0, qa * GRID_W:(qa + 1) * GRID_W, :], lo)
        start = pl.multiple_of(r0 * GRID_W, GRID_W)
        starts.append(start)
        scores.append([_dot_nt(qs, k_ref[0, pl.ds(start, win), :]) + tb_ref[0, off], _dot_nt(qs, kc)])
    for qa in range(rows_per_step):
        o = _softmax_pv(scores[qa], [v_ref[0, pl.ds(starts[qa], win), :], vc])
        o = jnp.where(lo, o[:GRID_W], o[GRID_W:])
        o_ref[0, qa * GRID_W:(qa + 1) * GRID_W, :] = o.astype(o_ref.dtype)


def _na_attention(qkv, qkv_c, tb):
    b, t, _ = qkv.shape
    ctx = qkv_c.shape[1]
    rows = t // GRID_W
    rps = 8
    hp = NA_HEADS // 2
    tq = rps * GRID_W
    return pl.pallas_call(
        functools.partial(_na_kernel, rows=rows, rows_per_step=rps),
        grid=(b, hp, rows // rps),
        in_specs=[
            pl.BlockSpec((1, tq, LANES), lambda bi, h, i: (bi, i, h)),
            pl.BlockSpec((1, t, LANES), lambda bi, h, i: (bi, 0, hp + h)),
            pl.BlockSpec((1, t, LANES), lambda bi, h, i: (bi, 0, 2 * hp + h)),
            pl.BlockSpec((1, ctx, LANES), lambda bi, h, i: (bi, 0, hp + h)),
            pl.BlockSpec((1, ctx, LANES), lambda bi, h, i: (bi, 0, 2 * hp + h)),
            pl.BlockSpec((1, NA_WIN_ROWS, 2 * GRID_W, NA_WIN_ROWS * GRID_W), lambda bi, h, i: (h, 0, 0, 0)),
        ],
        out_specs=pl.BlockSpec((1, tq, LANES), lambda bi, h, i: (bi, i, h)),
        out_shape=jax.ShapeDtypeStruct((b, t, hp * LANES), BF16),
        compiler_params=_params("parallel", "parallel", "arbitrary"),
        name="na_attention",
    )(qkv, qkv, qkv, qkv_c, qkv_c, tb)


def _ctx_na_kernel(q_ref, k_ref, v_ref, o_ref):
    n = q_ref.shape[1]
    lo = lax.broadcasted_iota(jnp.int32, (n, LANES), 1) < NA_HEAD_DIM
    o = _softmax_pv([_dot_nt(_stack_heads(q_ref[0], lo), k_ref[0])], [v_ref[0]])
    o_ref[0] = jnp.where(lo, o[:n], o[n:]).astype(o_ref.dtype)


def _ctx_na_attention(qkv_c):
    b, ctx, _ = qkv_c.shape
    hp = NA_HEADS // 2
    return pl.pallas_call(
        _ctx_na_kernel,
        grid=(b, hp),
        in_specs=[
            pl.BlockSpec((1, ctx, LANES), lambda bi, h: (bi, 0, h)),
            pl.BlockSpec((1, ctx, LANES), lambda bi, h: (bi, 0, hp + h)),
            pl.BlockSpec((1, ctx, LANES), lambda bi, h: (bi, 0, 2 * hp + h)),
        ],
        out_specs=pl.BlockSpec((1, ctx, LANES), lambda bi, h: (bi, 0, h)),
        out_shape=jax.ShapeDtypeStruct((b, ctx, hp * LANES), BF16),
        compiler_params=_params("parallel", "parallel"),
        name="ctx_na_attention",
    )(qkv_c, qkv_c, qkv_c)


def _lam_full(lam_ref, lam_init):
    lv = lam_ref[...]
    s1 = jnp.sum(lv[0:1] * lv[1:2], axis=-1, keepdims=True)
    s2 = jnp.sum(lv[2:3] * lv[3:4], axis=-1, keepdims=True)
    return jnp.exp(s1) - jnp.exp(s2) + lam_init


def _diff_post(o, sub_ref, lam_init):
    y = o * lax.rsqrt(jnp.mean(o * o, axis=-1, keepdims=True) + EPS)
    return y * sub_ref[...] * (1.0 - lam_init)


def _diff_kernel(q_ref, kc_ref, vc_ref, *rest, tk, n_lat, lam_init, nstream):
    if n_lat:
        k_ref, v_ref, lam_ref, sub_ref, o_ref, vx_ref, sa_ref, sb_ref, vcx_ref, m_ref, acc_ref = rest
    else:
        lam_ref, sub_ref, o_ref, vcx_ref, m_ref, acc_ref = rest
    tq = q_ref.shape[1] // nstream
    lo = lax.broadcasted_iota(jnp.int32, (tq, LANES), 1) < DIFF_HEAD_DIM
    qs = jnp.concatenate([_stack_heads(q_ref[0, s * tq:(s + 1) * tq, :], lo) for s in range(nstream)], axis=0)
    nrow = qs.shape[0]
    halves = [slice(j * nrow // 2, (j + 1) * nrow // 2) for j in range(2)]

    @pl.when(pl.program_id(2) == 0)
    def _():
        vcx_ref[:, :LANES] = vc_ref[0]
        vcx_ref[:, LANES:] = jnp.ones((vcx_ref.shape[0], LANES), BF16)
        if n_lat:
            vx_ref[:, :LANES] = v_ref[0]
            vx_ref[:, LANES:] = jnp.ones((vx_ref.shape[0], LANES), BF16)

    def absorb(s, vxb, rows, first=False):
        m_old = m_ref[rows, :]
        m_new = s.max(axis=-1, keepdims=True) if first else jnp.maximum(m_old, s.max(axis=-1, keepdims=True))
        pv = _dot(jnp.exp2((s - m_new).astype(BF16)), vxb)
        acc_ref[rows, :] = pv if first else jnp.exp2(m_old - m_new) * acc_ref[rows, :] + pv
        m_ref[rows, :] = m_new

    s_ctx = _dot_nt(qs, kc_ref[0])
    if not n_lat:
        absorb(s_ctx, vcx_ref[...], slice(0, nrow), first=True)
    else:
        def lat_k(c):
            return k_ref[0, pl.ds(pl.multiple_of(c * tk, tk), tk), :]

        def lat_vx(c):
            return vx_ref[pl.ds(pl.multiple_of(c * tk, tk), tk), :]

        def phase(cur_ref, nxt_ref, c, prefetch):
            if prefetch:
                nxt_ref[...] = _dot_nt(qs, lat_k(c + 1))
            for rows in halves:
                absorb(cur_ref[rows, :], lat_vx(c), rows)

        sa_ref[...] = _dot_nt(qs, lat_k(0))
        absorb(s_ctx, vcx_ref[...], slice(0, nrow), first=True)

        def body(j, carry):
            phase(sa_ref, sb_ref, 2 * j, True)
            phase(sb_ref, sa_ref, 2 * j + 1, True)
            return carry

        lax.fori_loop(0, n_lat // 2 - 1, body, 0)
        phase(sa_ref, sb_ref, n_lat - 2, True)
        phase(sb_ref, sa_ref, n_lat - 1, False)
    lam_full = _lam_full(lam_ref, lam_init)
    for s in range(nstream):
        acc = acc_ref[2 * s * tq:2 * (s + 1) * tq, :]
        o = acc[:, :LANES] / acc[:, LANES:LANES + 1]
        o = o[:tq] - lam_full * o[tq:]
        o_ref[0, s * tq:(s + 1) * tq, :] = _diff_post(o, sub_ref, lam_init).astype(o_ref.dtype)


def _diff_attention(q_src, qkv_c, qkv, lam, subln, lam_init, tq, tk, nstream=1):
    b, tqs, _ = q_src.shape
    ctx = qkv_c.shape[1]
    h0 = 3 * NA_HEADS // 2
    nh = DIFF_HEADS
    in_specs = [
        pl.BlockSpec((1, tq, LANES), lambda bi, h, i: (bi, i, h0 + h)),
        pl.BlockSpec((1, ctx, LANES), lambda bi, h, i: (bi, 0, h0 + nh + h)),
        pl.BlockSpec((1, ctx, LANES), lambda bi, h, i: (bi, 0, h0 + 2 * nh + h)),
    ]
    args = [q_src, qkv_c, qkv_c]
    n_lat = 0
    scratch = []
    if qkv is not None:
        t = qkv.shape[1]
        n_lat = t // tk
        in_specs += [
            pl.BlockSpec((1, t, LANES), lambda bi, h, i: (bi, 0, h0 + nh + h)),
            pl.BlockSpec((1, t, LANES), lambda bi, h, i: (bi, 0, h0 + 2 * nh + h)),
        ]
        args += [qkv, qkv]
        assert n_lat % 2 == 0
        scratch += [pltpu.VMEM((t, 2 * LANES), BF16), pltpu.VMEM((2 * tq, tk), F32), pltpu.VMEM((2 * tq, tk), F32)]
    scratch += [pltpu.VMEM((ctx, 2 * LANES), BF16), pltpu.VMEM((2 * tq, 1), F32), pltpu.VMEM((2 * tq, 2 * LANES), F32)]
    in_specs += [
        pl.BlockSpec(lam.shape, lambda bi, h, i: (0, 0)),
        pl.BlockSpec((1, LANES), lambda bi, h, i: (0, 0)),
    ]
    args += [lam.astype(F32), subln.astype(F32).reshape(1, LANES)]
    return pl.pallas_call(
        functools.partial(_diff_kernel, tk=tk, n_lat=n_lat, lam_init=lam_init, nstream=nstream),
        grid=(b, nh, tqs // tq),
        in_specs=in_specs,
        out_specs=pl.BlockSpec((1, tq, LANES), lambda bi, h, i: (bi, i, h)),
        out_shape=jax.ShapeDtypeStruct((b, tqs, nh * LANES), BF16),
        scratch_shapes=scratch,
        compiler_params=_params("parallel", "parallel", "arbitrary"),
        name="diff_attention",
    )(*args)


def _chunk_masks(n):
    row = lax.broadcasted_iota(jnp.int32, (n, n), 0)
    col = lax.broadcasted_iota(jnp.int32, (n, n), 1)
    shift = int(math.log2(CHUNK))
    same = lax.shift_right_logical(row, shift) == lax.shift_right_logical(col, shift)
    return row, col, same


def _chunk_scan(x, reverse, pos):
    n, w = x.shape
    s = 1
    while s < 8:
        if reverse:
            x = x + jnp.where(pos < CHUNK - s, pltpu.roll(x, n - s, 0), 0.0)
        else:
            x = x + jnp.where(pos >= s, pltpu.roll(x, s, 0), 0.0)
        s *= 2
    while s < CHUNK:
        parts = []
        for c in range(n // CHUNK):
            xc = x[c * CHUNK:(c + 1) * CHUNK]
            pad = jnp.zeros((s, w), F32)
            parts.append(xc + (jnp.concatenate([xc[s:], pad], axis=0) if reverse
                               else jnp.concatenate([pad, xc[:CHUNK - s]], axis=0)))
        x = jnp.concatenate(parts, axis=0)
        s *= 2
    return x


def _chunk_totals(G, reverse):
    n, w = G.shape
    r8 = lax.broadcasted_iota(jnp.int32, (8, w), 0)
    parts, dec = [], jnp.zeros((8, w), F32)
    for c in range(n // CHUNK):
        r = c * CHUNK + (0 if reverse else CHUNK - 1)
        tot = G[r:r + 1, :]
        parts.append(jnp.broadcast_to(tot, (CHUNK, w)))
        dec = jnp.where(r8 == c, jnp.broadcast_to(tot, (8, w)), dec)
    return jnp.concatenate(parts, axis=0), dec


def _rec_prep_kernel(dq_ref, dk_ref, dv_ref, pq_ref, pk_ref, pv_ref, nq_ref, nk_ref, nv_ref,
                     hq_ref, hi_ref, hff_ref, hfb_ref, dab_ref, cwq_ref, cwk_ref, cwv_ref, lb_ref, ab_ref,
                     cqe_ref, ckt_ref, coi_ref, cdec_ref, cv_ref,
                     dw_ref, du_ref, dkt_ref, dqe_ref, daqk_ref, ddec_ref, *, nblk):
    i = pl.program_id(1)
    n = REC_BLOCK
    w = REC_W
    row, col, same = _chunk_masks(PAIR)
    incl = (same & (row >= col), same & (row <= col))
    strict = (same & (row > col), same & (row < col))
    rid = lax.broadcasted_iota(jnp.int32, (n, w), 0)
    pos = rid & (CHUNK - 1)
    pos1 = pos[:, :LANES]
    pairs = [slice(p * PAIR, (p + 1) * PAIR) for p in range(n // PAIR)]
    heads = [slice(h * LANES, (h + 1) * LANES) for h in range(REC_HEADS)]
    keys = [(h, d, p) for h in range(REC_HEADS) for d in range(2) for p in range(len(pairs))]
    hp_keys = [(h, p) for h in range(REC_HEADS) for p in range(len(pairs))]

    cq = _silu(hq_ref[0].astype(F32)) * (REC_HEAD_DIM ** -0.5)
    cvb = hi_ref[0]
    cv_ref[0] = cvb
    qin, ktail = {}, {}
    for d, f_ref in enumerate((hff_ref, hfb_ref)):
        lbv = lb_ref[pl.ds(d, 1), :]
        sig = _sigmoid(f_ref[0])
        g = jnp.log(lbv + (1.0 - lbv) * sig)
        ck = (1.0 - lbv) * (1.0 - sig)
        G = _chunk_scan(g, d == 1, pos)
        Gl, dec = _chunk_totals(G, d == 1)
        ktail[d] = (ck * jnp.exp(Gl - G)).astype(BF16)
        qin[d] = (cq * jnp.exp(G - Gl)).astype(BF16)
        cqe_ref[d, 0] = (cq * jnp.exp(G)).astype(BF16)
        ckt_ref[d, 0] = ktail[d]
        cdec_ref[d, 0, 0] = jnp.exp(dec)
    amat = {(h, d, p): jnp.where(incl[d], _dot_nt(qin[d][pairs[p], heads[h]], ktail[d][pairs[p], heads[h]]), 0.0)
            for h, d, p in keys}
    for h, d, p in keys:
        coi_ref[d, 0, pairs[p], heads[h]] = _dot(amat[h, d, p].astype(BF16), cvb[pairs[p], heads[h]])

    last = BF16_ROWS - 1

    def conv_silu(x_ref, p_ref, n_ref, cw_ref):
        x = x_ref[0].astype(F32)
        cw = cw_ref[...]
        prev_row = jnp.where(i > 0, p_ref[0, last:last + 1, :].astype(F32), 0.0)
        next_row = jnp.where(i < nblk - 1, n_ref[0, 0:1, :].astype(F32), 0.0)
        xp = jnp.where(rid == 0, prev_row, pltpu.roll(x, 1, 0))
        xn = jnp.where(rid == n - 1, next_row, pltpu.roll(x, n - 1, 0))
        return _silu(cw[0:1] * xp + cw[1:2] * x + cw[2:3] * xn)

    def l2n(x):
        return jnp.concatenate(
            [x[:, hs] * lax.rsqrt(jnp.sum(x[:, hs] * x[:, hs], axis=-1, keepdims=True) + EPS) for hs in heads], axis=1)

    qn = l2n(conv_silu(dq_ref, pq_ref, nq_ref, cwq_ref)) * (REC_HEAD_DIM ** -0.5)
    kn = l2n(conv_silu(dk_ref, pk_ref, nk_ref, cwk_ref))
    vv = conv_silu(dv_ref, pv_ref, nv_ref, cwv_ref)
    knb, qnb = kn.astype(BF16), qn.astype(BF16)
    kk = {(h, p): _dot_nt(knb[pairs[p], heads[h]], knb[pairs[p], heads[h]]) for h, p in hp_keys}
    qk = {(h, p): _dot_nt(qnb[pairs[p], heads[h]], knb[pairs[p], heads[h]]) for h, p in hp_keys}
    tile = dab_ref[0]
    arate, dtb = ab_ref[0:1, :], ab_ref[1:2, :]
    xs = tile + dtb
    gt = -arate * (jnp.maximum(xs, 0.0) + jnp.log(1.0 + jnp.exp(-jnp.abs(xs))))
    bt = _sigmoid(tile)
    lane = lax.broadcasted_iota(jnp.int32, (n, LANES), 1)
    spread = lambda t, j0: jnp.concatenate(
        [jnp.broadcast_to(jnp.sum(jnp.where(lane == j0 + h, t, 0.0), axis=-1, keepdims=True), (n, LANES))
         for h in range(REC_HEADS)], axis=1)
    eye = (row == col).astype(F32)
    rhs, gmat = {}, {}
    for d in range(2):
        gsc = _chunk_scan(gt, d == 1, pos1)
        Gc = spread(gsc, d * REC_HEADS)
        bb = spread(bt, (2 + d) * REC_HEADS)
        Gl, dec = _chunk_totals(Gc, d == 1)
        eG = jnp.exp(Gc)
        rhs[d] = ((kn * bb * eG).astype(BF16), (vv * bb).astype(BF16))
        for h, p in hp_keys:
            gcp = Gc[pairs[p], heads[h]]
            gamma = jnp.exp(jnp.where(incl[d], gcp - gcp.T, MASK_VALUE))
            gmat[h, d, p] = jnp.where(strict[d], bb[pairs[p], heads[h]] * kk[h, p] * gamma, 0.0)
            daqk_ref[d, 0, pairs[p], heads[h]] = jnp.where(incl[d], qk[h, p] * gamma, 0.0).astype(BF16)
        dkt_ref[d, 0] = (kn * jnp.exp(Gl - Gc)).astype(BF16)
        dqe_ref[d, 0] = (qn * eG).astype(BF16)
        ddec_ref[d, 0, 0] = jnp.exp(dec)

    T = {key: eye for key in keys}
    for k in range(int(math.log2(CHUNK))):
        blk = lax.shift_right_logical(row, k + 1) == lax.shift_right_logical(col, k + 1)
        sub = lax.shift_right_logical(row, k) != lax.shift_right_logical(col, k)
        ak = {key: jnp.where(blk & sub, gmat[key], 0.0) for key in keys}
        if k == 0:
            T = {key: T[key] - ak[key] for key in keys}
            continue
        Tb = {key: T[key].astype(BF16) for key in keys}
        x = {key: _dot(ak[key].astype(BF16), Tb[key]).astype(BF16) for key in keys}
        T = {key: T[key] - _dot(Tb[key], x[key]) for key in keys}
    for h, d, p in keys:
        both = jnp.concatenate([rhs[d][0][pairs[p], heads[h]], rhs[d][1][pairs[p], heads[h]]], axis=1)
        wu = _dot(T[h, d, p].astype(BF16), both)
        dw_ref[d, 0, pairs[p], heads[h]] = wu[:, :LANES].astype(BF16)
        du_ref[d, 0, pairs[p], heads[h]] = wu[:, LANES:].astype(BF16)


def _rec_prep(p16, p32, conv_w, lb, ab):
    b, t, _ = p16.shape
    nblk = t // REC_BLOCK
    n = REC_BLOCK
    w = REC_W
    per = n // BF16_ROWS
    main = lambda blk: pl.BlockSpec((1, n, w), lambda bi, i: (bi, i, blk))
    prev = lambda blk: pl.BlockSpec((1, BF16_ROWS, w), lambda bi, i: (bi, jnp.maximum(i * per - 1, 0), blk))
    nxt = lambda blk: pl.BlockSpec((1, BF16_ROWS, w),
                                   lambda bi, i: (bi, jnp.minimum((i + 1) * per, t // BF16_ROWS - 1), blk))
    taps = lambda blk: pl.BlockSpec((8, w), lambda bi, i: (0, blk))
    in_specs = [main(P16_DQ), main(P16_DK), main(P16_DV), prev(P16_DQ), prev(P16_DK), prev(P16_DV),
                nxt(P16_DQ), nxt(P16_DK), nxt(P16_DV), main(P16_HQ), main(P16_HI),
                main(P32_HFF), main(P32_HFB),
                pl.BlockSpec((1, n, LANES), lambda bi, i: (bi, i, P32_DAB_TILE)),
                taps(0), taps(1), taps(2),
                pl.BlockSpec((8, w), lambda bi, i: (0, 0)),
                pl.BlockSpec((8, LANES), lambda bi, i: (0, 0))]
    dir_tok = lambda dt: (jax.ShapeDtypeStruct((2, b, t, w), dt),
                          pl.BlockSpec((2, 1, n, w), lambda bi, i: (0, bi, i, 0)))
    dir_dec = (jax.ShapeDtypeStruct((2, b, nblk, 8, w), F32),
               pl.BlockSpec((2, 1, 1, 8, w), lambda bi, i: (0, bi, i, 0, 0)))
    outs = [dir_tok(BF16), dir_tok(BF16), dir_tok(F32), dir_dec,
            (jax.ShapeDtypeStruct((b, t, w), BF16), pl.BlockSpec((1, n, w), lambda bi, i: (bi, i, 0))),
            dir_tok(BF16), dir_tok(BF16), dir_tok(BF16), dir_tok(BF16), dir_tok(BF16), dir_dec]
    return pl.pallas_call(
        functools.partial(_rec_prep_kernel, nblk=nblk),
        grid=(b, nblk),
        in_specs=in_specs,
        out_specs=[o[1] for o in outs],
        out_shape=[o[0] for o in outs],
        compiler_params=_params("parallel", "arbitrary"),
        name="rec_prep",
    )(*([p16] * 11), p32, p32, p32, conv_w, conv_w, conv_w, lb, ab)


_N_PREP = 11


def _rec_scan_kernel(*refs, nblk, has_init, with_output):
    ins = [refs[d * _N_PREP:(d + 1) * _N_PREP] for d in range(2)]
    rest = list(refs[2 * _N_PREP:])
    s0c_ref, s0d_ref = (rest.pop(0), rest.pop(0)) if has_init else (None, None)
    if with_output:
        oc_refs, od_refs = (rest.pop(0), rest.pop(0)), (rest.pop(0), rest.pop(0))
    sfc_ref, sfd_ref, sc_ref, sd_ref = rest
    i = pl.program_id(1)

    @pl.when(i == 0)
    def _():
        if has_init:
            sc_ref[...] = s0c_ref[0]
            sd_ref[...] = s0d_ref[0]
        else:
            sc_ref[...] = jnp.zeros_like(sc_ref)
            sd_ref[...] = jnp.zeros_like(sd_ref)

    nchunk = REC_BLOCK // CHUNK
    keys = [(d, h) for d in range(2) for h in range(REC_HEADS)]
    for step in range(nchunk):
        chunk_of = lambda d: step if d == 0 else nchunk - 1 - step

        def tok(d, idx, h):
            c = chunk_of(d)
            ref = ins[d][idx]
            sl = (slice(c * CHUNK, (c + 1) * CHUNK), slice(h * LANES, (h + 1) * LANES))
            return ref[(0,) * (len(ref.shape) - 2) + sl]

        def dec(d, idx, h):
            c = chunk_of(d)
            return ins[d][idx][0, 0, 0, c:c + 1, h * LANES:(h + 1) * LANES]

        def out_slice(d, h):
            c = chunk_of(d)
            return (0, slice(c * CHUNK, (c + 1) * CHUNK), slice(h * LANES, (h + 1) * LANES))

        S = {k: sd_ref[k[0], k[1]] for k in keys}
        Sb = {k: S[k].astype(BF16) for k in keys}
        if with_output:
            both = {(d, h): _dot(jnp.concatenate([tok(d, 5, h), tok(d, 8, h)], axis=0), Sb[d, h]) for d, h in keys}
            ws = {k: both[k][:CHUNK] for k in keys}
            qs = {k: both[k][CHUNK:] for k in keys}
        else:
            ws = {(d, h): _dot(tok(d, 5, h), Sb[d, h]) for d, h in keys}
        ST = {k: sc_ref[k[0], k[1]] for k in keys}
        if with_output:
            for d, h in keys:
                oc_refs[d][out_slice(d, h)] = (
                    tok(d, 2, h) + _dot_nt(tok(d, 0, h), ST[d, h].astype(BF16))).astype(BF16)
        for d, h in keys:
            sc_ref[d, h] = ST[d, h] * dec(d, 3, h) + _dot_tn(tok(d, 4, h), tok(d, 1, h))
        vnb = {(d, h): (tok(d, 6, h).astype(F32) - ws[d, h]).astype(BF16) for d, h in keys}
        for d, h in keys:
            sd_ref[d, h] = S[d, h] * dec(d, 10, h) + _dot_tn(tok(d, 7, h), vnb[d, h])
        if with_output:
            for d, h in keys:
                vpair = jnp.concatenate([vnb[d, h], vnb[d, h]], axis=0)
                od_refs[d][out_slice(d, h)] = (qs[d, h] + _dot(tok(d, 9, h), vpair)).astype(BF16)

    @pl.when(i == nblk - 1)
    def _():
        sfc_ref[0] = sc_ref[...]
        sfd_ref[0] = sd_ref[...]


def _rec_scan(prep, init, with_output):
    _, b, t, w = prep[0].shape
    nblk = t // REC_BLOCK
    n = REC_BLOCK
    in_specs, args = [], []
    for d in range(2):
        blk = (lambda i: i) if d == 0 else (lambda i: nblk - 1 - i)
        tok = pl.BlockSpec((1, 1, n, w), lambda bi, i, d=d, blk=blk: (d, bi, blk(i), 0))
        dec = pl.BlockSpec((1, 1, 1, 8, w), lambda bi, i, d=d, blk=blk: (d, bi, blk(i), 0, 0))
        cvs = pl.BlockSpec((1, n, w), lambda bi, i, blk=blk: (bi, blk(i), 0))
        in_specs += [tok, tok, tok, dec, cvs, tok, tok, tok, tok, tok, dec]
        args += list(prep)
    state = pl.BlockSpec((1, 2, REC_HEADS, LANES, LANES), lambda bi, i: (bi, 0, 0, 0, 0))
    if init is not None:
        in_specs += [state, state]
        args += list(init)
    out_specs, out_shape = [], []
    if with_output:
        fwd = pl.BlockSpec((1, n, w), lambda bi, i: (bi, i, 0))
        bwd = pl.BlockSpec((1, n, w), lambda bi, i: (bi, nblk - 1 - i, 0))
        out_specs += [fwd, bwd, fwd, bwd]
        out_shape += [jax.ShapeDtypeStruct((b, t, w), BF16)] * 4
    out_specs += [state, state]
    out_shape += [jax.ShapeDtypeStruct((b, 2, REC_HEADS, LANES, LANES), F32)] * 2
    res = pl.pallas_call(
        functools.partial(_rec_scan_kernel, nblk=nblk, has_init=init is not None, with_output=with_output),
        grid=(b, nblk),
        in_specs=in_specs,
        out_specs=out_specs,
        out_shape=out_shape,
        scratch_shapes=[pltpu.VMEM((2, REC_HEADS, LANES, LANES), F32)] * 2,
        compiler_params=_params("parallel", "arbitrary"),
        name="rec_scan",
    )(*args)
    return (res[:4], (res[4], res[5])) if with_output else (None, (res[0], res[1]))


def _gated_head_norm(f_ref, b_ref, gate_ref, nw_ref):
    o = f_ref[0].astype(F32) + b_ref[0].astype(F32)
    gate = _silu(gate_ref[0].astype(F32))
    parts = []
    for h in range(REC_HEADS):
        oh = o[:, h * LANES:(h + 1) * LANES]
        parts.append(oh * lax.rsqrt(jnp.mean(oh * oh, axis=-1, keepdims=True) + EPS) * nw_ref[...])
    return (jnp.concatenate(parts, axis=1) * gate).astype(BF16)


def _tail_kernel(*refs, recurrent, final, f_splits):
    refs = list(refs)
    x_ref = refs.pop(0)
    if recurrent:
        ocf, ocb, odf, odb, cg, dg, cn, dn = [refs.pop(0) for _ in range(8)]
        ya, yb = _gated_head_norm(ocf, ocb, cg, cn), _gated_head_norm(odf, odb, dg, dn)
    else:
        ya, yb = refs.pop(0)[0], refs.pop(0)[0]
    wa_ref, wb_ref, gm_ref, nw_ref, sh_ref, sc_ref, gf_ref, wg_ref, wu_ref, wd_ref = refs[:10]
    o_ref = refs[-1]
    x1 = x_ref[0] + gm_ref[0] * (_dot(ya, wa_ref[...]) + _dot(yb, wb_ref[...]))
    h = _rms_mod(x1, nw_ref[...], sh_ref[0], sc_ref[0]).astype(BF16)
    acc = None
    for f0, f1 in f_splits:
        z = (_silu(_dot(h, wg_ref[:, f0:f1])) * _dot(h, wu_ref[:, f0:f1])).astype(BF16)
        part = _dot(z, wd_ref[f0:f1, :])
        acc = part if acc is None else acc + part
    out = x1 + gf_ref[0] * acc
    if final:
        out = out * lax.rsqrt(jnp.mean(out * out, axis=-1, keepdims=True) + EPS) * refs[10][...]
    o_ref[0] = out


def _tail(x, mix_inputs, wa, wb, g_mix, nw, shift, scale, g_ffn, wg, wu, wd, final_norm, tm, norms=None):
    b, t, d = x.shape
    f = wg.shape[1]
    recurrent = norms is not None
    tokw = lambda width, blk=0: pl.BlockSpec((1, tm, width), lambda bi, i: (bi, i, blk))
    vec = lambda width: pl.BlockSpec((1, width), lambda bi, i: (0, 0))
    in_specs = [tokw(d)]
    args = [x]
    if recurrent:
        *outs, p16 = mix_inputs
        in_specs += [tokw(REC_W)] * 4 + [tokw(REC_W, P16_HGATE), tokw(REC_W, P16_DGATE), vec(LANES), vec(LANES)]
        args += list(outs) + [p16, p16, norms[0].reshape(1, LANES), norms[1].reshape(1, LANES)]
    else:
        in_specs += [tokw(a.shape[2]) for a in mix_inputs]
        args += list(mix_inputs)
    in_specs += [_resident(wa.shape), _resident(wb.shape), _mod_spec(g_mix, d), vec(d),
                 _mod_spec(shift, d), _mod_spec(scale, d), _mod_spec(g_ffn, d),
                 _resident((d, f)), _resident((d, f)), _resident((f, d))]
    args += [wa, wb, g_mix, nw.reshape(1, d), shift, scale, g_ffn, wg, wu, wd]
    if final_norm is not None:
        in_specs.append(vec(d))
        args.append(final_norm.reshape(1, d))
    mxu = 2 * LANES
    f_mid = -(-f // (2 * mxu)) * mxu
    return pl.pallas_call(
        functools.partial(_tail_kernel, recurrent=recurrent, final=final_norm is not None,
                          f_splits=((0, f_mid), (f_mid, f))),
        grid=(b, t // tm),
        in_specs=in_specs,
        out_specs=pl.BlockSpec((1, tm, d), lambda bi, i: (bi, i, 0)),
        out_shape=jax.ShapeDtypeStruct((b, t, d), F32),
        compiler_params=_params("parallel", "parallel"),
        name="layer_tail",
    )(*args)


def _pad_rows(a, rows):
    return jnp.pad(a, ((0, rows - a.shape[0]),) + ((0, 0),) * (a.ndim - 1))


def kernel(x, c, ctx, c_ctx, ada_w, ada_b, norm_mix, norm_ffn, w_mix_out, ffn_gate, ffn_up, ffn_down,
           att_w_in, att_rpb, att_lambda, att_subln, rec_w_in, rec_lb_logits, rec_conv_w, rec_a_log,
           rec_dt_bias, rec_c_norm, rec_d_norm, final_norm):
    b, t, d = x.shape
    n_ctx = ctx.shape[1]
    depth = ada_w.shape[0]
    assert depth == 2 and t % (8 * GRID_W) == 0 and n_ctx % REC_BLOCK == 0
    tm = 512 if t % 512 == 0 else 256
    tmc = 256

    nrow = -(-(b + 1) // 8) * 8
    mods = _adaln(_pad_rows(jnp.concatenate([c, c_ctx[None, :]], axis=0).astype(F32), nrow), ada_w, ada_b)

    def layer_mods(layer):
        m = mods[layer].reshape(nrow, 6, 1, d)
        return [m[:b, k] for k in range(6)], [m[b:b + 1, k] for k in range(6)]

    bf = lambda a: a.astype(BF16)
    h_ctx = ctx

    ml, mc = layer_mods(0)
    na_w, df_w = NA_HEADS * NA_HEAD_DIM, 2 * DIFF_HEADS * DIFF_HEAD_DIM
    qscale = jnp.ones((att_w_in.shape[2],), F32)
    qscale = qscale.at[:na_w].set(NA_HEAD_DIM ** -0.5 * LOG2E).at[3 * na_w:3 * na_w + df_w].set(DIFF_HEAD_DIM ** -0.5 * LOG2E)
    w_in = bf(att_w_in[0] * qscale[None, :])
    rope = _rope_tables(t)
    qkv = _proj(x, norm_mix[0], ml[0], ml[1], w_in, BF16, tm, 512, rope=rope, rope_tiles=(3, 4))
    qkv_c = _proj(h_ctx, norm_mix[0], mc[0], mc[1], w_in, BF16, tmc, 512)
    lam_init = 0.8 - 0.6 * math.exp(-0.3 * 0)
    tb = _na_bias_table(att_rpb[0])
    y_na = _na_attention(qkv, qkv_c, tb)
    y_df = _diff_attention(qkv, qkv_c, qkv, att_lambda[0], att_subln[0], lam_init, tq=1024, tk=min(2048, t // 2), nstream=4)
    yc_na = _ctx_na_attention(qkv_c)
    yc_df = _diff_attention(qkv_c, qkv_c, None, att_lambda[0], att_subln[0], lam_init, tq=n_ctx, tk=0)
    wo = bf(w_mix_out[0])
    wg, wu, wd = bf(ffn_gate[0]), bf(ffn_up[0]), bf(ffn_down[0])
    x = _tail(x, (y_na, y_df), wo[:na_w], wo[na_w:], ml[2], norm_ffn[0], ml[3], ml[4], ml[5], wg, wu, wd, None, tm)
    h_ctx = _tail(h_ctx, (yc_na, yc_df), wo[:na_w], wo[na_w:], mc[2], norm_ffn[0], mc[3], mc[4], mc[5], wg, wu, wd,
                  None, tmc)

    ml, mc = layer_mods(1)
    hw = REC_W
    wr = rec_w_in[0]
    cuts = [0, hw, 2 * hw, 3 * hw, 4 * hw, 5 * hw, 8 * hw, 8 * hw + 4 * REC_HEADS, 9 * hw + 4 * REC_HEADS]
    hq, hff, hfb, hi, hgate, dqkv, dab, dgate = [wr[:, cuts[k]:cuts[k + 1]] for k in range(8)]
    w16 = bf(jnp.concatenate([dqkv, hq, hi, hgate, dgate], axis=1))
    w32 = bf(jnp.concatenate([hff, hfb, jnp.pad(dab, ((0, 0), (0, LANES - dab.shape[1])))], axis=1))
    lbs = jax.nn.softmax(rec_lb_logits.astype(F32), axis=0)
    lb = _pad_rows((jnp.cumsum(lbs, axis=0) - lbs[0])[1], 8)
    conv_w = _pad_rows(rec_conv_w[0].astype(F32), 8)
    a_rate = jnp.exp(rec_a_log[0].astype(F32)).reshape(-1)
    dtb = rec_dt_bias[0].astype(F32).reshape(-1)
    ab = jnp.zeros((8, LANES), F32).at[0, :a_rate.shape[0]].set(a_rate).at[1, :dtb.shape[0]].set(dtb)

    def rec_inputs(h, mods_, tile):
        p16 = _proj(h, norm_mix[1], mods_[0], mods_[1], w16, BF16, tile, 512)
        p32 = _proj(h, norm_mix[1], mods_[0], mods_[1], w32, F32, tile, w32.shape[1])
        return p16, _rec_prep(p16, p32, conv_w, lb, ab)

    _, prep_c = rec_inputs(h_ctx, mc, tmc)
    p16, prep_l = rec_inputs(x, ml, tm)
    _, s_ctx = _rec_scan(prep_c, None, with_output=False)
    outs, _ = _rec_scan(prep_l, s_ctx, with_output=True)
    wo = bf(w_mix_out[1])
    return _tail(x, (*outs, p16), wo[:hw], wo[hw:], ml[2], norm_ffn[1], ml[3], ml[4], ml[5],
                 bf(ffn_gate[1]), bf(ffn_up[1]), bf(ffn_down[1]), final_norm, tm,
                 norms=(rec_c_norm[0].astype(F32), rec_d_norm[0].astype(F32)))
```

```python
import functools
import math

import jax
import jax.numpy as jnp
from jax import lax
from jax.experimental import pallas as pl
from jax.experimental.pallas import tpu as pltpu

F32 = jnp.float32
BF16 = jnp.bfloat16

EPS = 1e-6
MASK_VALUE = -1e30
GRID_W = 64
NA_HEADS = 8
NA_HEAD_DIM = 64
NA_WIN_ROWS = 8
NA_WIN_COLS = 16
DIFF_HEADS = 4
DIFF_HEAD_DIM = 64
REC_HEADS = 4
REC_HEAD_DIM = 128
CHUNK = 64
ROPE_THETA = 10000.0
LOG2E = math.log2(math.e)
LANES = 128
BF16_ROWS = 16
PAIR = 2 * CHUNK
REC_BLOCK = 4 * CHUNK
REC_W = REC_HEADS * REC_HEAD_DIM
VMEM_LIMIT = 56 * 1024 * 1024

P16_DQ, P16_DK, P16_DV, P16_HQ, P16_HI, P16_HGATE, P16_DGATE = range(7)
P32_HFF, P32_HFB = 0, 1
P32_DAB_TILE = 2 * REC_HEADS

_NT = (((1,), (1,)), ((), ()))
_TN = (((0,), (0,)), ((), ()))


def _params(*sem):
    return pltpu.CompilerParams(dimension_semantics=sem, vmem_limit_bytes=VMEM_LIMIT)


def _resident(shape):
    return pl.BlockSpec(shape, lambda *_: (0,) * len(shape), pipeline_mode=pl.Buffered(1))


def _dot(a, b):
    return jnp.dot(a, b, preferred_element_type=F32)


def _dot_nt(a, b):
    return lax.dot_general(a, b, _NT, preferred_element_type=F32)


def _dot_tn(a, b):
    return lax.dot_general(a, b, _TN, preferred_element_type=F32)


def _mask01(m):
    return jnp.where(m, 1.0, 0.0).astype(BF16)


def _sigmoid(x):
    return 0.5 * jnp.tanh(0.5 * x) + 0.5


def _silu(x):
    h = 0.5 * x
    return h * jnp.tanh(h) + h


def _rms_mod(x, nw, shift, scale):
    y = x * lax.rsqrt(jnp.mean(x * x, axis=-1, keepdims=True) + EPS)
    return (y * nw) * (1.0 + scale) + shift


def _mod_spec(arr, d):
    per_batch = arr.shape[0] > 1
    return pl.BlockSpec((1, 1, d), (lambda bi, i: (bi, 0, 0)) if per_batch else (lambda bi, i: (0, 0, 0)))


def _adaln_kernel(c_ref, w_ref, b_ref, o_ref):
    a = _silu(c_ref[...])
    w = w_ref[0]
    a1 = a.astype(BF16)
    a2 = (a - a1.astype(F32)).astype(BF16)
    w1 = w.astype(BF16)
    w2 = (w - w1.astype(F32)).astype(BF16)
    o_ref[0] = _dot(a1, w1) + _dot(a1, w2) + _dot(a2, w1) + b_ref[0]


def _adaln(cvecs, ada_w, ada_b):
    depth, d, n = ada_w.shape
    rows = cvecs.shape[0]
    tn = 1536
    return pl.pallas_call(
        _adaln_kernel,
        grid=(depth, n // tn),
        in_specs=[
            pl.BlockSpec((rows, d), lambda l, j: (0, 0)),
            pl.BlockSpec((1, d, tn), lambda l, j: (l, 0, j)),
            pl.BlockSpec((1, 1, tn), lambda l, j: (l, 0, j)),
        ],
        out_specs=pl.BlockSpec((1, rows, tn), lambda l, j: (l, 0, j)),
        out_shape=jax.ShapeDtypeStruct((depth, rows, n), F32),
        compiler_params=_params("parallel", "parallel"),
        name="adaln",
    )(cvecs, ada_w, ada_b.reshape(depth, 1, n))


def _proj_kernel(x_ref, nw_ref, sh_ref, sc_ref, w_ref, *rest, tn, rope_tiles):
    if rope_tiles:
        cos_ref, sa_ref, sb_ref, o_ref = rest
    else:
        (o_ref,) = rest
    h = _rms_mod(x_ref[0], nw_ref[...], sh_ref[0], sc_ref[0]).astype(BF16)
    n = w_ref.shape[1]
    for j in range(n // tn):
        cols = slice(j * tn, (j + 1) * tn)
        acc = _dot(h, w_ref[:, cols])
        if j in rope_tiles:
            tile = lambda r: jnp.concatenate([r[...]] * (tn // LANES), axis=1)
            acc = (acc * tile(cos_ref)
                   + pltpu.roll(acc, 16, 1) * tile(sa_ref)
                   + pltpu.roll(acc, tn - 16, 1) * tile(sb_ref))
        o_ref[0, :, cols] = acc.astype(o_ref.dtype)


def _proj(x, nw, shift, scale, w, out_dtype, tm, tn, rope=None, rope_tiles=()):
    b, t, d = x.shape
    n = w.shape[1]
    in_specs = [
        pl.BlockSpec((1, tm, d), lambda bi, i: (bi, i, 0)),
        pl.BlockSpec((1, d), lambda bi, i: (0, 0)),
        _mod_spec(shift, d), _mod_spec(scale, d),
        _resident((d, n)),
    ]
    args = [x, nw.reshape(1, d), shift, scale, w]
    if rope_tiles:
        in_specs += [pl.BlockSpec((tm, LANES), lambda bi, i: (i, 0))] * 3
        args += list(rope)
    return pl.pallas_call(
        functools.partial(_proj_kernel, tn=tn, rope_tiles=tuple(rope_tiles)),
        grid=(b, t // tm),
        in_specs=in_specs,
        out_specs=pl.BlockSpec((1, tm, n), lambda bi, i: (bi, i, 0)),
        out_shape=jax.ShapeDtypeStruct((b, t, n), out_dtype),
        compiler_params=_params("parallel", "parallel"),
        name="norm_proj",
    )(*args)


def _rope_tables(t):
    pos = jnp.arange(t)
    rows, cols = (pos // GRID_W).astype(F32), (pos % GRID_W).astype(F32)
    quarter = DIFF_HEAD_DIM // 4
    inv_freq = ROPE_THETA ** (-jnp.arange(quarter, dtype=F32) / quarter)
    lane = jnp.arange(LANES)
    within = lane % DIFF_HEAD_DIM
    use_row = within < DIFF_HEAD_DIM // 2
    freq = inv_freq[within % quarter]
    ang = jnp.where(use_row[None, :], rows[:, None], cols[:, None]) * freq[None, :]
    cos, sin = jnp.cos(ang), jnp.sin(ang)
    second = ((within // quarter) % 2) == 1
    sa = jnp.where(second[None, :], sin, 0.0)
    sb = jnp.where(second[None, :], 0.0, -sin)
    return cos, sa, sb


def _na_bias_table(rpb):
    kw = NA_WIN_COLS
    col = jnp.arange(GRID_W)
    col_start = jnp.clip(col - kw // 2, 0, GRID_W - kw)
    col_ok = (col[None, :] >= col_start[:, None]) & (col[None, :] < col_start[:, None] + kw)
    col_idx = jnp.clip(col[None, :] - col[:, None] + kw - 1, 0, 2 * kw - 2)
    rpb_cols = jnp.where(col_ok[None, None], rpb.astype(F32)[:, :, col_idx], MASK_VALUE)
    idx = jnp.arange(NA_WIN_ROWS)[:, None] + jnp.arange(NA_WIN_ROWS)[None, :]
    tb = rpb_cols[:, idx] * LOG2E
    nh = rpb.shape[0]
    tb = tb.reshape(nh // 2, 2, NA_WIN_ROWS, NA_WIN_ROWS, GRID_W, GRID_W).transpose(0, 2, 1, 4, 3, 5)
    return tb.reshape(nh // 2, NA_WIN_ROWS, 2 * GRID_W, NA_WIN_ROWS * GRID_W)


def _softmax_pv(s_list, v_list):
    m = functools.reduce(jnp.maximum, [s.max(axis=-1, keepdims=True) for s in s_list])
    es = [jnp.exp2(s - m) for s in s_list]
    l = functools.reduce(jnp.add, [e.sum(axis=-1, keepdims=True) for e in es])
    o = functools.reduce(jnp.add, [_dot(e.astype(BF16), v) for e, v in zip(es, v_list)])
    return o / l


def _stack_heads(q, lo):
    return jnp.concatenate([q * _mask01(lo), q * _mask01(jnp.logical_not(lo))], axis=0)


def _na_kernel(q_ref, k_ref, v_ref, kc_ref, vc_ref, tb_ref, o_ref, *, rows, rows_per_step):
    i = pl.program_id(2)
    lo = lax.broadcasted_iota(jnp.int32, (GRID_W, LANES), 1) < NA_HEAD_DIM
    kc, vc = kc_ref[0], vc_ref[0]
    win = NA_WIN_ROWS * GRID_W

    starts, scores = [], []
    for qa in range(rows_per_step):
        r = i * rows_per_step + qa
        r0 = jnp.clip(r - NA_WIN_ROWS // 2, 0, rows - NA_WIN_ROWS)
        off = r0 - r + NA_WIN_ROWS - 1
        qs = _stack_heads(q_ref[0, qa * GRID_W:(qa + 1) * GRID_W, :], lo)
        start = pl.multiple_of(r0 * GRID_W, GRID_W)
        starts.append(start)
        scores.append([_dot_nt(qs, k_ref[0, pl.ds(start, win), :]) + tb_ref[0, off], _dot_nt(qs, kc)])
    for qa in range(rows_per_step):
        o = _softmax_pv(scores[qa], [v_ref[0, pl.ds(starts[qa], win), :], vc])
        o = jnp.where(lo, o[:GRID_W], o[GRID_W:])
        o_ref[0, qa * GRID_W:(qa + 1) * GRID_W, :] = o.astype(o_ref.dtype)


def _na_attention(qkv, qkv_c, tb):
    b, t, _ = qkv.shape
    ctx = qkv_c.shape[1]
    rows = t // GRID_W
    rps = 16 if rows % 16 == 0 else 8
    hp = NA_HEADS // 2
    tq = rps * GRID_W
    return pl.pallas_call(
        functools.partial(_na_kernel, rows=rows, rows_per_step=rps),
        grid=(b, hp, rows // rps),
        in_specs=[
            pl.BlockSpec((1, tq, LANES), lambda bi, h, i: (bi, i, h)),
            pl.BlockSpec((1, t, LANES), lambda bi, h, i: (bi, 0, hp + h)),
            pl.BlockSpec((1, t, LANES), lambda bi, h, i: (bi, 0, 2 * hp + h)),
            pl.BlockSpec((1, ctx, LANES), lambda bi, h, i: (bi, 0, hp + h)),
            pl.BlockSpec((1, ctx, LANES), lambda bi, h, i: (bi, 0, 2 * hp + h)),
            pl.BlockSpec((1, NA_WIN_ROWS, 2 * GRID_W, NA_WIN_ROWS * GRID_W), lambda bi, h, i: (h, 0, 0, 0)),
        ],
        out_specs=pl.BlockSpec((1, tq, LANES), lambda bi, h, i: (bi, i, h)),
        out_shape=jax.ShapeDtypeStruct((b, t, hp * LANES), BF16),
        compiler_params=_params("parallel", "parallel", "arbitrary"),
        name="na_attention",
    )(qkv, qkv, qkv, qkv_c, qkv_c, tb)


def _ctx_na_kernel(q_ref, k_ref, v_ref, o_ref):
    n = q_ref.shape[1]
    lo = lax.broadcasted_iota(jnp.int32, (n, LANES), 1) < NA_HEAD_DIM
    o = _softmax_pv([_dot_nt(_stack_heads(q_ref[0], lo), k_ref[0])], [v_ref[0]])
    o_ref[0] = jnp.where(lo, o[:n], o[n:]).astype(o_ref.dtype)


def _ctx_na_attention(qkv_c):
    b, ctx, _ = qkv_c.shape
    hp = NA_HEADS // 2
    return pl.pallas_call(
        _ctx_na_kernel,
        grid=(b, hp),
        in_specs=[
            pl.BlockSpec((1, ctx, LANES), lambda bi, h: (bi, 0, h)),
            pl.BlockSpec((1, ctx, LANES), lambda bi, h: (bi, 0, hp + h)),
            pl.BlockSpec((1, ctx, LANES), lambda bi, h: (bi, 0, 2 * hp + h)),
        ],
        out_specs=pl.BlockSpec((1, ctx, LANES), lambda bi, h: (bi, 0, h)),
        out_shape=jax.ShapeDtypeStruct((b, ctx, hp * LANES), BF16),
        compiler_params=_params("parallel", "parallel"),
        name="ctx_na_attention",
    )(qkv_c, qkv_c, qkv_c)


def _lam_full(lam_ref, lam_init):
    lv = lam_ref[...]
    s1 = jnp.sum(lv[0:1] * lv[1:2], axis=-1, keepdims=True)
    s2 = jnp.sum(lv[2:3] * lv[3:4], axis=-1, keepdims=True)
    return jnp.exp(s1) - jnp.exp(s2) + lam_init


def _diff_post(o, sub_ref, lam_init):
    y = o * lax.rsqrt(jnp.mean(o * o, axis=-1, keepdims=True) + EPS)
    return y * sub_ref[...] * (1.0 - lam_init)


def _diff_kernel(q_ref, kc_ref, vc_ref, *rest, tk, n_lat, lam_init, nstream):
    if n_lat:
        k_ref, v_ref, lam_ref, sub_ref, o_ref, vx_ref, sa_ref, sb_ref, vcx_ref, m_ref, acc_ref = rest
    else:
        lam_ref, sub_ref, o_ref, vcx_ref, m_ref, acc_ref = rest
    tq = q_ref.shape[1] // nstream
    lo = lax.broadcasted_iota(jnp.int32, (tq, LANES), 1) < DIFF_HEAD_DIM
    qs = jnp.concatenate([_stack_heads(q_ref[0, s * tq:(s + 1) * tq, :], lo) for s in range(nstream)], axis=0)
    nrow = qs.shape[0]
    halves = [slice(j * nrow // 2, (j + 1) * nrow // 2) for j in range(2)]

    @pl.when(pl.program_id(2) == 0)
    def _():
        vcx_ref[:, :LANES] = vc_ref[0]
        vcx_ref[:, LANES:] = jnp.ones((vcx_ref.shape[0], LANES), BF16)
        if n_lat:
            vx_ref[:, :LANES] = v_ref[0]
            vx_ref[:, LANES:] = jnp.ones((vx_ref.shape[0], LANES), BF16)

    def absorb(s, vxb, rows, first=False):
        m_old = m_ref[rows, :]
        m_new = s.max(axis=-1, keepdims=True) if first else jnp.maximum(m_old, s.max(axis=-1, keepdims=True))
        pv = _dot(jnp.exp2((s - m_new).astype(BF16)), vxb)
        acc_ref[rows, :] = pv if first else jnp.exp2(m_old - m_new) * acc_ref[rows, :] + pv
        m_ref[rows, :] = m_new

    s_ctx = _dot_nt(qs, kc_ref[0])
    if not n_lat:
        absorb(s_ctx, vcx_ref[...], slice(0, nrow), first=True)
    else:
        def lat_k(c):
            return k_ref[0, pl.ds(pl.multiple_of(c * tk, tk), tk), :]

        def lat_vx(c):
            return vx_ref[pl.ds(pl.multiple_of(c * tk, tk), tk), :]

        def phase(cur_ref, nxt_ref, c, prefetch):
            if prefetch:
                nxt_ref[...] = _dot_nt(qs, lat_k(c + 1))
            for rows in halves:
                absorb(cur_ref[rows, :], lat_vx(c), rows)

        sa_ref[...] = _dot_nt(qs, lat_k(0))
        absorb(s_ctx, vcx_ref[...], slice(0, nrow), first=True)

        def body(j, carry):
            phase(sa_ref, sb_ref, 2 * j, True)
            phase(sb_ref, sa_ref, 2 * j + 1, True)
            return carry

        lax.fori_loop(0, n_lat // 2 - 1, body, 0)
        phase(sa_ref, sb_ref, n_lat - 2, True)
        phase(sb_ref, sa_ref, n_lat - 1, False)
    lam_full = _lam_full(lam_ref, lam_init)
    for s in range(nstream):
        acc = acc_ref[2 * s * tq:2 * (s + 1) * tq, :]
        o = acc[:, :LANES] / acc[:, LANES:LANES + 1]
        o = o[:tq] - lam_full * o[tq:]
        o_ref[0, s * tq:(s + 1) * tq, :] = _diff_post(o, sub_ref, lam_init).astype(o_ref.dtype)


def _diff_attention(q_src, qkv_c, qkv, lam, subln, lam_init, tq, tk, nstream=1):
    b, tqs, _ = q_src.shape
    ctx = qkv_c.shape[1]
    h0 = 3 * NA_HEADS // 2
    nh = DIFF_HEADS
    in_specs = [
        pl.BlockSpec((1, tq, LANES), lambda bi, h, i: (bi, i, h0 + h)),
        pl.BlockSpec((1, ctx, LANES), lambda bi, h, i: (bi, 0, h0 + nh + h)),
        pl.BlockSpec((1, ctx, LANES), lambda bi, h, i: (bi, 0, h0 + 2 * nh + h)),
    ]
    args = [q_src, qkv_c, qkv_c]
    n_lat = 0
    scratch = []
    if qkv is not None:
        t = qkv.shape[1]
        n_lat = t // tk
        in_specs += [
            pl.BlockSpec((1, t, LANES), lambda bi, h, i: (bi, 0, h0 + nh + h)),
            pl.BlockSpec((1, t, LANES), lambda bi, h, i: (bi, 0, h0 + 2 * nh + h)),
        ]
        args += [qkv, qkv]
        assert n_lat % 2 == 0
        scratch += [pltpu.VMEM((t, 2 * LANES), BF16), pltpu.VMEM((2 * tq, tk), F32), pltpu.VMEM((2 * tq, tk), F32)]
    scratch += [pltpu.VMEM((ctx, 2 * LANES), BF16), pltpu.VMEM((2 * tq, 1), F32), pltpu.VMEM((2 * tq, 2 * LANES), F32)]
    in_specs += [
        pl.BlockSpec(lam.shape, lambda bi, h, i: (0, 0)),
        pl.BlockSpec((1, LANES), lambda bi, h, i: (0, 0)),
    ]
    args += [lam.astype(F32), subln.astype(F32).reshape(1, LANES)]
    return pl.pallas_call(
        functools.partial(_diff_kernel, tk=tk, n_lat=n_lat, lam_init=lam_init, nstream=nstream),
        grid=(b, nh, tqs // tq),
        in_specs=in_specs,
        out_specs=pl.BlockSpec((1, tq, LANES), lambda bi, h, i: (bi, i, h)),
        out_shape=jax.ShapeDtypeStruct((b, tqs, nh * LANES), BF16),
        scratch_shapes=scratch,
        compiler_params=_params("parallel", "parallel", "arbitrary"),
        name="diff_attention",
    )(*args)


def _chunk_masks(n):
    row = lax.broadcasted_iota(jnp.int32, (n, n), 0)
    col = lax.broadcasted_iota(jnp.int32, (n, n), 1)
    shift = int(math.log2(CHUNK))
    same = lax.shift_right_logical(row, shift) == lax.shift_right_logical(col, shift)
    return row, col, same


def _chunk_scan(x, reverse, pos):
    n, w = x.shape
    s = 1
    while s < 8:
        if reverse:
            x = x + jnp.where(pos < CHUNK - s, pltpu.roll(x, n - s, 0), 0.0)
        else:
            x = x + jnp.where(pos >= s, pltpu.roll(x, s, 0), 0.0)
        s *= 2
    while s < CHUNK:
        parts = []
        for c in range(n // CHUNK):
            xc = x[c * CHUNK:(c + 1) * CHUNK]
            pad = jnp.zeros((s, w), F32)
            parts.append(xc + (jnp.concatenate([xc[s:], pad], axis=0) if reverse
                               else jnp.concatenate([pad, xc[:CHUNK - s]], axis=0)))
        x = jnp.concatenate(parts, axis=0)
        s *= 2
    return x


def _chunk_totals(G, reverse):
    n, w = G.shape
    r8 = lax.broadcasted_iota(jnp.int32, (8, w), 0)
    parts, dec = [], jnp.zeros((8, w), F32)
    for c in range(n // CHUNK):
        r = c * CHUNK + (0 if reverse else CHUNK - 1)
        tot = G[r:r + 1, :]
        parts.append(jnp.broadcast_to(tot, (CHUNK, w)))
        dec = jnp.where(r8 == c, jnp.broadcast_to(tot, (8, w)), dec)
    return jnp.concatenate(parts, axis=0), dec


def _rec_prep_kernel(dq_ref, dk_ref, dv_ref, pq_ref, pk_ref, pv_ref, nq_ref, nk_ref, nv_ref,
                     hq_ref, hi_ref, hff_ref, hfb_ref, dab_ref, cwq_ref, cwk_ref, cwv_ref, lb_ref, ab_ref,
                     cqe_ref, ckt_ref, coi_ref, cdec_ref, cv_ref,
                     dw_ref, du_ref, dkt_ref, dqe_ref, daqk_ref, ddec_ref, *, nblk):
    i = pl.program_id(1)
    n = REC_BLOCK
    w = REC_W
    row, col, same = _chunk_masks(PAIR)
    incl = (same & (row >= col), same & (row <= col))
    strict = (same & (row > col), same & (row < col))
    rid = lax.broadcasted_iota(jnp.int32, (n, w), 0)
    pos = rid & (CHUNK - 1)
    pos1 = pos[:, :LANES]
    pairs = [slice(p * PAIR, (p + 1) * PAIR) for p in range(n // PAIR)]
    heads = [slice(h * LANES, (h + 1) * LANES) for h in range(REC_HEADS)]
    keys = [(h, d, p) for h in range(REC_HEADS) for d in range(2) for p in range(len(pairs))]
    hp_keys = [(h, p) for h in range(REC_HEADS) for p in range(len(pairs))]

    cq = _silu(hq_ref[0].astype(F32)) * (REC_HEAD_DIM ** -0.5)
    cvb = hi_ref[0]
    cv_ref[0] = cvb
    qin, ktail = {}, {}
    for d, f_ref in enumerate((hff_ref, hfb_ref)):
        lbv = lb_ref[pl.ds(d, 1), :]
        sig = _sigmoid(f_ref[0])
        g = jnp.log(lbv + (1.0 - lbv) * sig) * LOG2E
        ck = (1.0 - lbv) * (1.0 - sig)
        G = _chunk_scan(g, d == 1, pos)
        Gl, dec = _chunk_totals(G, d == 1)
        ktail[d] = (ck * jnp.exp2(Gl - G)).astype(BF16)
        qin[d] = (cq * jnp.exp2(G - Gl)).astype(BF16)
        cqe_ref[d, 0] = (cq * jnp.exp2(G)).astype(BF16)
        ckt_ref[d, 0] = ktail[d]
        cdec_ref[d, 0, 0] = jnp.exp2(dec)
    amat = {(h, d, p): jnp.where(incl[d], _dot_nt(qin[d][pairs[p], heads[h]], ktail[d][pairs[p], heads[h]]), 0.0)
            for h, d, p in keys}
    for h, d, p in keys:
        coi_ref[d, 0, pairs[p], heads[h]] = _dot(amat[h, d, p].astype(BF16), cvb[pairs[p], heads[h]])

    last = BF16_ROWS - 1

    def conv_silu(x_ref, p_ref, n_ref, cw_ref):
        x = x_ref[0].astype(F32)
        cw = cw_ref[...]
        prev_row = jnp.where(i > 0, p_ref[0, last:last + 1, :].astype(F32), 0.0)
        next_row = jnp.where(i < nblk - 1, n_ref[0, 0:1, :].astype(F32), 0.0)
        xp = jnp.where(rid == 0, prev_row, pltpu.roll(x, 1, 0))
        xn = jnp.where(rid == n - 1, next_row, pltpu.roll(x, n - 1, 0))
        return _silu(cw[0:1] * xp + cw[1:2] * x + cw[2:3] * xn)

    def l2n(x):
        return jnp.concatenate(
            [x[:, hs] * lax.rsqrt(jnp.sum(x[:, hs] * x[:, hs], axis=-1, keepdims=True) + EPS) for hs in heads], axis=1)

    qn = l2n(conv_silu(dq_ref, pq_ref, nq_ref, cwq_ref)) * (REC_HEAD_DIM ** -0.5)
    kn = l2n(conv_silu(dk_ref, pk_ref, nk_ref, cwk_ref))
    vv = conv_silu(dv_ref, pv_ref, nv_ref, cwv_ref)
    knb, qnb = kn.astype(BF16), qn.astype(BF16)
    kk = {(h, p): _dot_nt(knb[pairs[p], heads[h]], knb[pairs[p], heads[h]]) for h, p in hp_keys}
    qk = {(h, p): _dot_nt(qnb[pairs[p], heads[h]], knb[pairs[p], heads[h]]) for h, p in hp_keys}
    tile = dab_ref[0]
    arate, dtb = ab_ref[0:1, :], ab_ref[1:2, :]
    xs = tile + dtb
    gt = -arate * (jnp.maximum(xs, 0.0) + jnp.log(1.0 + jnp.exp(-jnp.abs(xs))))
    bt = _sigmoid(tile)
    lane = lax.broadcasted_iota(jnp.int32, (n, LANES), 1)
    spread = lambda t, j0: jnp.concatenate(
        [jnp.broadcast_to(jnp.sum(jnp.where(lane == j0 + h, t, 0.0), axis=-1, keepdims=True), (n, LANES))
         for h in range(REC_HEADS)], axis=1)
    eye = (row == col).astype(F32)
    rhs, gmat = {}, {}
    for d in range(2):
        gsc = _chunk_scan(gt, d == 1, pos1)
        Gc = spread(gsc, d * REC_HEADS)
        bb = spread(bt, (2 + d) * REC_HEADS)
        Gl, dec = _chunk_totals(Gc, d == 1)
        eG = jnp.exp2(Gc)
        rhs[d] = ((kn * bb * eG).astype(BF16), (vv * bb).astype(BF16))
        for h, p in hp_keys:
            gcp = Gc[pairs[p], heads[h]]
            gamma = jnp.exp2(jnp.where(incl[d], gcp - gcp.T, MASK_VALUE))
            gmat[h, d, p] = jnp.where(strict[d], bb[pairs[p], heads[h]] * kk[h, p] * gamma, 0.0)
            daqk_ref[d, 0, pairs[p], heads[h]] = jnp.where(incl[d], qk[h, p] * gamma, 0.0).astype(BF16)
        dkt_ref[d, 0] = (kn * jnp.exp2(Gl - Gc)).astype(BF16)
        dqe_ref[d, 0] = (qn * eG).astype(BF16)
        ddec_ref[d, 0, 0] = jnp.exp2(dec)

    T = {key: eye for key in keys}
    for k in range(int(math.log2(CHUNK))):
        blk = lax.shift_right_logical(row, k + 1) == lax.shift_right_logical(col, k + 1)
        sub = lax.shift_right_logical(row, k) != lax.shift_right_logical(col, k)
        ak = {key: jnp.where(blk & sub, gmat[key], 0.0) for key in keys}
        if k == 0:
            T = {key: T[key] - ak[key] for key in keys}
            continue
        Tb = {key: T[key].astype(BF16) for key in keys}
        x = {key: _dot(ak[key].astype(BF16), Tb[key]).astype(BF16) for key in keys}
        T = {key: T[key] - _dot(Tb[key], x[key]) for key in keys}
    for h, d, p in keys:
        both = jnp.concatenate([rhs[d][0][pairs[p], heads[h]], rhs[d][1][pairs[p], heads[h]]], axis=1)
        wu = _dot(T[h, d, p].astype(BF16), both)
        dw_ref[d, 0, pairs[p], heads[h]] = wu[:, :LANES].astype(BF16)
        du_ref[d, 0, pairs[p], heads[h]] = wu[:, LANES:].astype(BF16)


def _rec_prep(p16, p32, conv_w, lb, ab):
    b, t, _ = p16.shape
    nblk = t // REC_BLOCK
    n = REC_BLOCK
    w = REC_W
    per = n // BF16_ROWS
    main = lambda blk: pl.BlockSpec((1, n, w), lambda bi, i: (bi, i, blk))
    prev = lambda blk: pl.BlockSpec((1, BF16_ROWS, w), lambda bi, i: (bi, jnp.maximum(i * per - 1, 0), blk))
    nxt = lambda blk: pl.BlockSpec((1, BF16_ROWS, w),
                                   lambda bi, i: (bi, jnp.minimum((i + 1) * per, t // BF16_ROWS - 1), blk))
    taps = lambda blk: pl.BlockSpec((8, w), lambda bi, i: (0, blk))
    in_specs = [main(P16_DQ), main(P16_DK), main(P16_DV), prev(P16_DQ), prev(P16_DK), prev(P16_DV),
                nxt(P16_DQ), nxt(P16_DK), nxt(P16_DV), main(P16_HQ), main(P16_HI),
                main(P32_HFF), main(P32_HFB),
                pl.BlockSpec((1, n, LANES), lambda bi, i: (bi, i, P32_DAB_TILE)),
                taps(0), taps(1), taps(2),
                pl.BlockSpec((8, w), lambda bi, i: (0, 0)),
                pl.BlockSpec((8, LANES), lambda bi, i: (0, 0))]
    dir_tok = lambda dt: (jax.ShapeDtypeStruct((2, b, t, w), dt),
                          pl.BlockSpec((2, 1, n, w), lambda bi, i: (0, bi, i, 0)))
    dir_dec = (jax.ShapeDtypeStruct((2, b, nblk, 8, w), F32),
               pl.BlockSpec((2, 1, 1, 8, w), lambda bi, i: (0, bi, i, 0, 0)))
    outs = [dir_tok(BF16), dir_tok(BF16), dir_tok(F32), dir_dec,
            (jax.ShapeDtypeStruct((b, t, w), BF16), pl.BlockSpec((1, n, w), lambda bi, i: (bi, i, 0))),
            dir_tok(BF16), dir_tok(BF16), dir_tok(BF16), dir_tok(BF16), dir_tok(BF16), dir_dec]
    return pl.pallas_call(
        functools.partial(_rec_prep_kernel, nblk=nblk),
        grid=(b, nblk),
        in_specs=in_specs,
        out_specs=[o[1] for o in outs],
        out_shape=[o[0] for o in outs],
        compiler_params=_params("parallel", "arbitrary"),
        name="rec_prep",
    )(*([p16] * 11), p32, p32, p32, conv_w, conv_w, conv_w, lb, ab)


_N_PREP = 11


def _rec_scan_kernel(*refs, nblk, has_init, with_output):
    ins = [refs[d * _N_PREP:(d + 1) * _N_PREP] for d in range(2)]
    rest = list(refs[2 * _N_PREP:])
    s0c_ref, s0d_ref = (rest.pop(0), rest.pop(0)) if has_init else (None, None)
    if with_output:
        oc_refs, od_refs = (rest.pop(0), rest.pop(0)), (rest.pop(0), rest.pop(0))
    sfc_ref, sfd_ref, sc_ref, sd_ref = rest
    i = pl.program_id(1)

    @pl.when(i == 0)
    def _():
        if has_init:
            sc_ref[...] = s0c_ref[0]
            sd_ref[...] = s0d_ref[0]
        else:
            sc_ref[...] = jnp.zeros_like(sc_ref)
            sd_ref[...] = jnp.zeros_like(sd_ref)

    nchunk = REC_BLOCK // CHUNK
    keys = [(d, h) for d in range(2) for h in range(REC_HEADS)]
    for step in range(nchunk):
        chunk_of = lambda d: step if d == 0 else nchunk - 1 - step

        def tok(d, idx, h):
            c = chunk_of(d)
            ref = ins[d][idx]
            sl = (slice(c * CHUNK, (c + 1) * CHUNK), slice(h * LANES, (h + 1) * LANES))
            return ref[(0,) * (len(ref.shape) - 2) + sl]

        def dec(d, idx, h):
            c = chunk_of(d)
            return ins[d][idx][0, 0, 0, c:c + 1, h * LANES:(h + 1) * LANES]

        def out_slice(d, h):
            c = chunk_of(d)
            return (0, slice(c * CHUNK, (c + 1) * CHUNK), slice(h * LANES, (h + 1) * LANES))

        S = {k: sd_ref[k[0], k[1]] for k in keys}
        Sb = {k: S[k].astype(BF16) for k in keys}
        if with_output:
            both = {(d, h): _dot(jnp.concatenate([tok(d, 5, h), tok(d, 8, h)], axis=0), Sb[d, h]) for d, h in keys}
            ws = {k: both[k][:CHUNK] for k in keys}
            qs = {k: both[k][CHUNK:] for k in keys}
        else:
            ws = {(d, h): _dot(tok(d, 5, h), Sb[d, h]) for d, h in keys}
        ST = {k: sc_ref[k[0], k[1]] for k in keys}
        if with_output:
            for d, h in keys:
                oc_refs[d][out_slice(d, h)] = (
                    tok(d, 2, h) + _dot_nt(tok(d, 0, h), ST[d, h].astype(BF16))).astype(BF16)
        for d, h in keys:
            sc_ref[d, h] = ST[d, h] * dec(d, 3, h) + _dot_tn(tok(d, 4, h), tok(d, 1, h))
        vnb = {(d, h): (tok(d, 6, h).astype(F32) - ws[d, h]).astype(BF16) for d, h in keys}
        for d, h in keys:
            sd_ref[d, h] = S[d, h] * dec(d, 10, h) + _dot_tn(tok(d, 7, h), vnb[d, h])
        if with_output:
            for d, h in keys:
                vpair = jnp.concatenate([vnb[d, h], vnb[d, h]], axis=0)
                od_refs[d][out_slice(d, h)] = (qs[d, h] + _dot(tok(d, 9, h), vpair)).astype(BF16)

    @pl.when(i == nblk - 1)
    def _():
        sfc_ref[0] = sc_ref[...]
        sfd_ref[0] = sd_ref[...]


def _rec_scan(prep, init, with_output):
    _, b, t, w = prep[0].shape
    nblk = t // REC_BLOCK
    n = REC_BLOCK
    in_specs, args = [], []
    for d in range(2):
        blk = (lambda i: i) if d == 0 else (lambda i: nblk - 1 - i)
        tok = pl.BlockSpec((1, 1, n, w), lambda bi, i, d=d, blk=blk: (d, bi, blk(i), 0))
        dec = pl.BlockSpec((1, 1, 1, 8, w), lambda bi, i, d=d, blk=blk: (d, bi, blk(i), 0, 0))
        cvs = pl.BlockSpec((1, n, w), lambda bi, i, blk=blk: (bi, blk(i), 0))
        in_specs += [tok, tok, tok, dec, cvs, tok, tok, tok, tok, tok, dec]
        args += list(prep)
    state = pl.BlockSpec((1, 2, REC_HEADS, LANES, LANES), lambda bi, i: (bi, 0, 0, 0, 0))
    if init is not None:
        in_specs += [state, state]
        args += list(init)
    out_specs, out_shape = [], []
    if with_output:
        fwd = pl.BlockSpec((1, n, w), lambda bi, i: (bi, i, 0))
        bwd = pl.BlockSpec((1, n, w), lambda bi, i: (bi, nblk - 1 - i, 0))
        out_specs += [fwd, bwd, fwd, bwd]
        out_shape += [jax.ShapeDtypeStruct((b, t, w), BF16)] * 4
    out_specs += [state, state]
    out_shape += [jax.ShapeDtypeStruct((b, 2, REC_HEADS, LANES, LANES), F32)] * 2
    res = pl.pallas_call(
        functools.partial(_rec_scan_kernel, nblk=nblk, has_init=init is not None, with_output=with_output),
        grid=(b, nblk),
        in_specs=in_specs,
        out_specs=out_specs,
        out_shape=out_shape,
        scratch_shapes=[pltpu.VMEM((2, REC_HEADS, LANES, LANES), F32)] * 2,
        compiler_params=_params("parallel", "arbitrary"),
        name="rec_scan",
    )(*args)
    return (res[:4], (res[4], res[5])) if with_output else (None, (res[0], res[1]))


def _gated_head_norm(f_ref, b_ref, gate_ref, nw_ref):
    o = f_ref[0].astype(F32) + b_ref[0].astype(F32)
    gate = _silu(gate_ref[0].astype(F32))
    parts = []
    for h in range(REC_HEADS):
        oh = o[:, h * LANES:(h + 1) * LANES]
        parts.append(oh * lax.rsqrt(jnp.mean(oh * oh, axis=-1, keepdims=True) + EPS) * nw_ref[...])
    return (jnp.concatenate(parts, axis=1) * gate).astype(BF16)


def _tail_kernel(*refs, recurrent, final, f_splits):
    refs = list(refs)
    x_ref = refs.pop(0)
    if recurrent:
        ocf, ocb, odf, odb, cg, dg, cn, dn = [refs.pop(0) for _ in range(8)]
        ya, yb = _gated_head_norm(ocf, ocb, cg, cn), _gated_head_norm(odf, odb, dg, dn)
    else:
        ya, yb = refs.pop(0)[0], refs.pop(0)[0]
    wa_ref, wb_ref, gm_ref, nw_ref, sh_ref, sc_ref, gf_ref, wg_ref, wu_ref, wd_ref = refs[:10]
    o_ref = refs[-1]
    x1 = x_ref[0] + gm_ref[0] * (_dot(ya, wa_ref[...]) + _dot(yb, wb_ref[...]))
    h = _rms_mod(x1, nw_ref[...], sh_ref[0], sc_ref[0]).astype(BF16)
    acc = None
    for f0, f1 in f_splits:
        z = (_silu(_dot(h, wg_ref[:, f0:f1])) * _dot(h, wu_ref[:, f0:f1])).astype(BF16)
        part = _dot(z, wd_ref[f0:f1, :])
        acc = part if acc is None else acc + part
    out = x1 + gf_ref[0] * acc
    if final:
        out = out * lax.rsqrt(jnp.mean(out * out, axis=-1, keepdims=True) + EPS) * refs[10][...]
    o_ref[0] = out


def _tail(x, mix_inputs, wa, wb, g_mix, nw, shift, scale, g_ffn, wg, wu, wd, final_norm, tm, norms=None):
    b, t, d = x.shape
    f = wg.shape[1]
    recurrent = norms is not None
    tokw = lambda width, blk=0: pl.BlockSpec((1, tm, width), lambda bi, i: (bi, i, blk))
    vec = lambda width: pl.BlockSpec((1, width), lambda bi, i: (0, 0))
    in_specs = [tokw(d)]
    args = [x]
    if recurrent:
        *outs, p16 = mix_inputs
        in_specs += [tokw(REC_W)] * 4 + [tokw(REC_W, P16_HGATE), tokw(REC_W, P16_DGATE), vec(LANES), vec(LANES)]
        args += list(outs) + [p16, p16, norms[0].reshape(1, LANES), norms[1].reshape(1, LANES)]
    else:
        in_specs += [tokw(a.shape[2]) for a in mix_inputs]
        args += list(mix_inputs)
    in_specs += [_resident(wa.shape), _resident(wb.shape), _mod_spec(g_mix, d), vec(d),
                 _mod_spec(shift, d), _mod_spec(scale, d), _mod_spec(g_ffn, d),
                 _resident((d, f)), _resident((d, f)), _resident((f, d))]
    args += [wa, wb, g_mix, nw.reshape(1, d), shift, scale, g_ffn, wg, wu, wd]
    if final_norm is not None:
        in_specs.append(vec(d))
        args.append(final_norm.reshape(1, d))
    mxu = 2 * LANES
    f_mid = -(-f // (2 * mxu)) * mxu
    return pl.pallas_call(
        functools.partial(_tail_kernel, recurrent=recurrent, final=final_norm is not None,
                          f_splits=((0, f_mid), (f_mid, f))),
        grid=(b, t // tm),
        in_specs=in_specs,
        out_specs=pl.BlockSpec((1, tm, d), lambda bi, i: (bi, i, 0)),
        out_shape=jax.ShapeDtypeStruct((b, t, d), F32),
        compiler_params=_params("parallel", "parallel"),
        name="layer_tail",
    )(*args)


def _pad_rows(a, rows):
    return jnp.pad(a, ((0, rows - a.shape[0]),) + ((0, 0),) * (a.ndim - 1))


def kernel(x, c, ctx, c_ctx, ada_w, ada_b, norm_mix, norm_ffn, w_mix_out, ffn_gate, ffn_up, ffn_down,
           att_w_in, att_rpb, att_lambda, att_subln, rec_w_in, rec_lb_logits, rec_conv_w, rec_a_log,
           rec_dt_bias, rec_c_norm, rec_d_norm, final_norm):
    b, t, d = x.shape
    n_ctx = ctx.shape[1]
    depth = ada_w.shape[0]
    assert depth == 2 and t % (8 * GRID_W) == 0 and n_ctx % REC_BLOCK == 0
    tm = 512 if t % 512 == 0 else 256
    tmp = 1024 if t % 1024 == 0 else tm
    tmc = 256

    nrow = -(-(b + 1) // 8) * 8
    mods = _adaln(_pad_rows(jnp.concatenate([c, c_ctx[None, :]], axis=0).astype(F32), nrow), ada_w, ada_b)

    def layer_mods(layer):
        m = mods[layer].reshape(nrow, 6, 1, d)
        return [m[:b, k] for k in range(6)], [m[b:b + 1, k] for k in range(6)]

    bf = lambda a: a.astype(BF16)
    h_ctx = ctx

    ml, mc = layer_mods(0)
    na_w, df_w = NA_HEADS * NA_HEAD_DIM, 2 * DIFF_HEADS * DIFF_HEAD_DIM
    qscale = jnp.ones((att_w_in.shape[2],), F32)
    qscale = qscale.at[:na_w].set(NA_HEAD_DIM ** -0.5 * LOG2E).at[3 * na_w:3 * na_w + df_w].set(DIFF_HEAD_DIM ** -0.5 * LOG2E)
    w_in = bf(att_w_in[0] * qscale[None, :])
    rope = _rope_tables(t)
    qkv = _proj(x, norm_mix[0], ml[0], ml[1], w_in, BF16, tmp, 512, rope=rope, rope_tiles=(3, 4))
    qkv_c = _proj(h_ctx, norm_mix[0], mc[0], mc[1], w_in, BF16, tmc, 512)
    lam_init = 0.8 - 0.6 * math.exp(-0.3 * 0)
    tb = _na_bias_table(att_rpb[0])
    y_na = _na_attention(qkv, qkv_c, tb)
    y_df = _diff_attention(qkv, qkv_c, qkv, att_lambda[0], att_subln[0], lam_init, tq=1024, tk=min(2048, t // 2), nstream=4)
    yc_na = _ctx_na_attention(qkv_c)
    yc_df = _diff_attention(qkv_c, qkv_c, None, att_lambda[0], att_subln[0], lam_init, tq=n_ctx, tk=0)
    wo = bf(w_mix_out[0])
    wg, wu, wd = bf(ffn_gate[0]), bf(ffn_up[0]), bf(ffn_down[0])
    x = _tail(x, (y_na, y_df), wo[:na_w], wo[na_w:], ml[2], norm_ffn[0], ml[3], ml[4], ml[5], wg, wu, wd, None, tm)
    h_ctx = _tail(h_ctx, (yc_na, yc_df), wo[:na_w], wo[na_w:], mc[2], norm_ffn[0], mc[3], mc[4], mc[5], wg, wu, wd,
                  None, tmc)

    ml, mc = layer_mods(1)
    hw = REC_W
    wr = rec_w_in[0]
    cuts = [0, hw, 2 * hw, 3 * hw, 4 * hw, 5 * hw, 8 * hw, 8 * hw + 4 * REC_HEADS, 9 * hw + 4 * REC_HEADS]
    hq, hff, hfb, hi, hgate, dqkv, dab, dgate = [wr[:, cuts[k]:cuts[k + 1]] for k in range(8)]
    w16 = bf(jnp.concatenate([dqkv, hq, hi, hgate, dgate], axis=1))
    w32 = bf(jnp.concatenate([hff, hfb, jnp.pad(dab, ((0, 0), (0, LANES - dab.shape[1])))], axis=1))
    lbs = jax.nn.softmax(rec_lb_logits.astype(F32), axis=0)
    lb = _pad_rows((jnp.cumsum(lbs, axis=0) - lbs[0])[1], 8)
    conv_w = _pad_rows(rec_conv_w[0].astype(F32), 8)
    a_rate = (jnp.exp(rec_a_log[0].astype(F32)) * LOG2E).reshape(-1)
    dtb = rec_dt_bias[0].astype(F32).reshape(-1)
    ab = jnp.zeros((8, LANES), F32).at[0, :a_rate.shape[0]].set(a_rate).at[1, :dtb.shape[0]].set(dtb)

    def rec_inputs(h, mods_, tile):
        p16 = _proj(h, norm_mix[1], mods_[0], mods_[1], w16, BF16, tile, 512)
        p32 = _proj(h, norm_mix[1], mods_[0], mods_[1], w32, F32, tile, w32.shape[1])
        return p16, _rec_prep(p16, p32, conv_w, lb, ab)

    _, prep_c = rec_inputs(h_ctx, mc, tmc)
    p16, prep_l = rec_inputs(x, ml, tmp)
    _, s_ctx = _rec_scan(prep_c, None, with_output=False)
    outs, _ = _rec_scan(prep_l, s_ctx, with_output=True)
    wo = bf(w_mix_out[1])
    return _tail(x, (*outs, p16), wo[:hw], wo[hw:], ml[2], norm_ffn[1], ml[3], ml[4], ml[5],
                 bf(ffn_gate[1]), bf(ffn_up[1]), bf(ffn_down[1]), final_norm, tm,
                 norms=(rec_c_norm[0].astype(F32), rec_d_norm[0].astype(F32)))
```

```python
import functools
import math

import jax
import jax.numpy as jnp
import numpy as np
from jax import lax
from jax.experimental import pallas as pl
from jax.experimental.pallas import tpu as pltpu

F32 = jnp.float32
BF16 = jnp.bfloat16

EPS = 1e-6
MASK_VALUE = -1e30
GRID_W = 64
NA_HEADS = 8
NA_HEAD_DIM = 64
NA_WIN_ROWS = 8
NA_WIN_COLS = 16
DIFF_HEADS = 4
DIFF_HEAD_DIM = 64
REC_HEADS = 4
REC_HEAD_DIM = 128
CHUNK = 64
ROPE_THETA = 10000.0
LOG2E = math.log2(math.e)
LANES = 128
BF16_ROWS = 16
PAIR = 2 * CHUNK
REC_BLOCK = 4 * CHUNK
REC_W = REC_HEADS * REC_HEAD_DIM
VMEM_LIMIT = 56 * 1024 * 1024

P16_DQ, P16_DK, P16_DV, P16_HQ, P16_HI, P16_HGATE, P16_DGATE = range(7)
P32_HFF, P32_HFB = 0, 1
P32_DAB_TILE = 2 * REC_HEADS

_NT = (((1,), (1,)), ((), ()))
_TN = (((0,), (0,)), ((), ()))


def _params(*sem):
    return pltpu.CompilerParams(dimension_semantics=sem, vmem_limit_bytes=VMEM_LIMIT)


def _resident(shape):
    return pl.BlockSpec(shape, lambda *_: (0,) * len(shape), pipeline_mode=pl.Buffered(1))


def _dot(a, b):
    return jnp.dot(a, b, preferred_element_type=F32)


def _dot_nt(a, b):
    return lax.dot_general(a, b, _NT, preferred_element_type=F32)


def _dot_tn(a, b):
    return lax.dot_general(a, b, _TN, preferred_element_type=F32)


def _mask01(m):
    return jnp.where(m, 1.0, 0.0).astype(BF16)


def _sigmoid(x):
    return 0.5 * jnp.tanh(0.5 * x) + 0.5


def _silu(x):
    h = 0.5 * x
    return h * jnp.tanh(h) + h


def _rms_mod(x, nw, shift, scale):
    y = x * lax.rsqrt(jnp.mean(x * x, axis=-1, keepdims=True) + EPS)
    return (y * nw) * (1.0 + scale) + shift


def _mod_spec(arr, d):
    per_batch = arr.shape[0] > 1
    return pl.BlockSpec((1, 1, d), (lambda bi, i: (bi, 0, 0)) if per_batch else (lambda bi, i: (0, 0, 0)))


def _adaln_kernel(c_ref, w_ref, b_ref, o_ref):
    a = _silu(c_ref[...])
    w = w_ref[0]
    a1 = a.astype(BF16)
    a2 = (a - a1.astype(F32)).astype(BF16)
    w1 = w.astype(BF16)
    w2 = (w - w1.astype(F32)).astype(BF16)
    o_ref[0] = _dot(a1, w1) + _dot(a1, w2) + _dot(a2, w1) + b_ref[0]


def _adaln(cvecs, ada_w, ada_b):
    depth, d, n = ada_w.shape
    rows = cvecs.shape[0]
    tn = 1536
    return pl.pallas_call(
        _adaln_kernel,
        grid=(depth, n // tn),
        in_specs=[
            pl.BlockSpec((rows, d), lambda l, j: (0, 0)),
            pl.BlockSpec((1, d, tn), lambda l, j: (l, 0, j)),
            pl.BlockSpec((1, 1, tn), lambda l, j: (l, 0, j)),
        ],
        out_specs=pl.BlockSpec((1, rows, tn), lambda l, j: (l, 0, j)),
        out_shape=jax.ShapeDtypeStruct((depth, rows, n), F32),
        compiler_params=_params("parallel", "parallel"),
        name="adaln",
    )(cvecs, ada_w, ada_b.reshape(depth, 1, n))


def _proj_kernel(x_ref, nw_ref, sh_ref, sc_ref, w_ref, *rest, tn, rope_tiles):
    if rope_tiles:
        cos_ref, sa_ref, sb_ref, o_ref = rest
    else:
        (o_ref,) = rest
    h = _rms_mod(x_ref[0], nw_ref[...], sh_ref[0], sc_ref[0]).astype(BF16)
    n = w_ref.shape[1]
    for j in range(n // tn):
        cols = slice(j * tn, (j + 1) * tn)
        acc = _dot(h, w_ref[:, cols])
        if j in rope_tiles:
            tile = lambda r: jnp.concatenate([r[...]] * (tn // LANES), axis=1)
            acc = (acc * tile(cos_ref)
                   + pltpu.roll(acc, 16, 1) * tile(sa_ref)
                   + pltpu.roll(acc, tn - 16, 1) * tile(sb_ref))
        o_ref[0, :, cols] = acc.astype(o_ref.dtype)


def _proj(x, nw, shift, scale, w, out_dtype, tm, tn, rope=None, rope_tiles=()):
    b, t, d = x.shape
    n = w.shape[1]
    in_specs = [
        pl.BlockSpec((1, tm, d), lambda bi, i: (bi, i, 0)),
        pl.BlockSpec((1, d), lambda bi, i: (0, 0)),
        _mod_spec(shift, d), _mod_spec(scale, d),
        _resident((d, n)),
    ]
    args = [x, nw.reshape(1, d), shift, scale, w]
    if rope_tiles:
        in_specs += [pl.BlockSpec((tm, LANES), lambda bi, i: (i, 0))] * 3
        args += list(rope)
    return pl.pallas_call(
        functools.partial(_proj_kernel, tn=tn, rope_tiles=tuple(rope_tiles)),
        grid=(b, t // tm),
        in_specs=in_specs,
        out_specs=pl.BlockSpec((1, tm, n), lambda bi, i: (bi, i, 0)),
        out_shape=jax.ShapeDtypeStruct((b, t, n), out_dtype),
        compiler_params=_params("parallel", "parallel"),
        name="norm_proj",
    )(*args)


def _rope_tables(t):
    pos = jnp.arange(t)
    rows, cols = (pos // GRID_W).astype(F32), (pos % GRID_W).astype(F32)
    quarter = DIFF_HEAD_DIM // 4
    inv_freq = ROPE_THETA ** (-jnp.arange(quarter, dtype=F32) / quarter)
    lane = jnp.arange(LANES)
    within = lane % DIFF_HEAD_DIM
    use_row = within < DIFF_HEAD_DIM // 2
    freq = inv_freq[within % quarter]
    ang = jnp.where(use_row[None, :], rows[:, None], cols[:, None]) * freq[None, :]
    cos, sin = jnp.cos(ang), jnp.sin(ang)
    second = ((within // quarter) % 2) == 1
    sa = jnp.where(second[None, :], sin, 0.0)
    sb = jnp.where(second[None, :], 0.0, -sin)
    return cos, sa, sb


def _na_bias_table(rpb):
    kw = NA_WIN_COLS
    col = np.arange(GRID_W)
    col_start = np.clip(col - kw // 2, 0, GRID_W - kw)
    col_ok = (col[None, :] >= col_start[:, None]) & (col[None, :] < col_start[:, None] + kw)
    col_idx = np.clip(col[None, :] - col[:, None] + kw - 1, 0, 2 * kw - 2)
    onehot = (col_idx[None] == np.arange(2 * kw - 1)[:, None, None]).astype(np.float32)
    rpb_cols = jnp.einsum("hrc,cqk->hrqk", rpb.astype(F32), onehot, precision=lax.Precision.HIGHEST)
    rpb_cols = jnp.where(col_ok[None, None], rpb_cols, MASK_VALUE)
    tb = jnp.stack([rpb_cols[:, o:o + NA_WIN_ROWS] for o in range(NA_WIN_ROWS)], axis=1) * LOG2E
    nh = rpb.shape[0]
    tb = tb.reshape(nh // 2, 2, NA_WIN_ROWS, NA_WIN_ROWS, GRID_W, GRID_W).transpose(0, 2, 1, 4, 3, 5)
    return tb.reshape(nh // 2, NA_WIN_ROWS, 2 * GRID_W, NA_WIN_ROWS * GRID_W)


def _softmax_pv(s_list, v_list):
    m = functools.reduce(jnp.maximum, [s.max(axis=-1, keepdims=True) for s in s_list])
    es = [jnp.exp2(s - m) for s in s_list]
    l = functools.reduce(jnp.add, [e.sum(axis=-1, keepdims=True) for e in es])
    o = functools.reduce(jnp.add, [_dot(e.astype(BF16), v) for e, v in zip(es, v_list)])
    return o / l


def _stack_heads(q, lo):
    return jnp.concatenate([q * _mask01(lo), q * _mask01(jnp.logical_not(lo))], axis=0)


def _na_kernel(q_ref, k_ref, v_ref, kc_ref, vc_ref, tb_ref, o_ref, *, rows, rows_per_step):
    i = pl.program_id(2)
    lo = lax.broadcasted_iota(jnp.int32, (GRID_W, LANES), 1) < NA_HEAD_DIM
    kc, vc = kc_ref[0], vc_ref[0]
    win = NA_WIN_ROWS * GRID_W

    starts, scores = [], []
    for qa in range(rows_per_step):
        r = i * rows_per_step + qa
        r0 = jnp.clip(r - NA_WIN_ROWS // 2, 0, rows - NA_WIN_ROWS)
        off = r0 - r + NA_WIN_ROWS - 1
        qs = _stack_heads(q_ref[0, qa * GRID_W:(qa + 1) * GRID_W, :], lo)
        start = pl.multiple_of(r0 * GRID_W, GRID_W)
        starts.append(start)
        scores.append([_dot_nt(qs, k_ref[0, pl.ds(start, win), :]) + tb_ref[0, off], _dot_nt(qs, kc)])
    for qa in range(rows_per_step):
        o = _softmax_pv(scores[qa], [v_ref[0, pl.ds(starts[qa], win), :], vc])
        o = jnp.where(lo, o[:GRID_W], o[GRID_W:])
        o_ref[0, qa * GRID_W:(qa + 1) * GRID_W, :] = o.astype(o_ref.dtype)


def _na_attention(qkv, qkv_c, tb):
    b, t, _ = qkv.shape
    ctx = qkv_c.shape[1]
    rows = t // GRID_W
    rps = 16 if rows % 16 == 0 else 8
    hp = NA_HEADS // 2
    tq = rps * GRID_W
    return pl.pallas_call(
        functools.partial(_na_kernel, rows=rows, rows_per_step=rps),
        grid=(b, hp, rows // rps),
        in_specs=[
            pl.BlockSpec((1, tq, LANES), lambda bi, h, i: (bi, i, h)),
            pl.BlockSpec((1, t, LANES), lambda bi, h, i: (bi, 0, hp + h)),
            pl.BlockSpec((1, t, LANES), lambda bi, h, i: (bi, 0, 2 * hp + h)),
            pl.BlockSpec((1, ctx, LANES), lambda bi, h, i: (bi, 0, hp + h)),
            pl.BlockSpec((1, ctx, LANES), lambda bi, h, i: (bi, 0, 2 * hp + h)),
            pl.BlockSpec((1, NA_WIN_ROWS, 2 * GRID_W, NA_WIN_ROWS * GRID_W), lambda bi, h, i: (h, 0, 0, 0)),
        ],
        out_specs=pl.BlockSpec((1, tq, LANES), lambda bi, h, i: (bi, i, h)),
        out_shape=jax.ShapeDtypeStruct((b, t, hp * LANES), BF16),
        compiler_params=_params("parallel", "parallel", "arbitrary"),
        name="na_attention",
    )(qkv, qkv, qkv, qkv_c, qkv_c, tb)


def _ctx_na_kernel(q_ref, k_ref, v_ref, o_ref):
    n = q_ref.shape[1]
    lo = lax.broadcasted_iota(jnp.int32, (n, LANES), 1) < NA_HEAD_DIM
    o = _softmax_pv([_dot_nt(_stack_heads(q_ref[0], lo), k_ref[0])], [v_ref[0]])
    o_ref[0] = jnp.where(lo, o[:n], o[n:]).astype(o_ref.dtype)


def _ctx_na_attention(qkv_c):
    b, ctx, _ = qkv_c.shape
    hp = NA_HEADS // 2
    return pl.pallas_call(
        _ctx_na_kernel,
        grid=(b, hp),
        in_specs=[
            pl.BlockSpec((1, ctx, LANES), lambda bi, h: (bi, 0, h)),
            pl.BlockSpec((1, ctx, LANES), lambda bi, h: (bi, 0, hp + h)),
            pl.BlockSpec((1, ctx, LANES), lambda bi, h: (bi, 0, 2 * hp + h)),
        ],
        out_specs=pl.BlockSpec((1, ctx, LANES), lambda bi, h: (bi, 0, h)),
        out_shape=jax.ShapeDtypeStruct((b, ctx, hp * LANES), BF16),
        compiler_params=_params("parallel", "parallel"),
        name="ctx_na_attention",
    )(qkv_c, qkv_c, qkv_c)


def _lam_full(lam_ref, lam_init):
    lv = lam_ref[...]
    s1 = jnp.sum(lv[0:1] * lv[1:2], axis=-1, keepdims=True)
    s2 = jnp.sum(lv[2:3] * lv[3:4], axis=-1, keepdims=True)
    return jnp.exp(s1) - jnp.exp(s2) + lam_init


def _diff_post(o, sub_ref, lam_init):
    y = o * lax.rsqrt(jnp.mean(o * o, axis=-1, keepdims=True) + EPS)
    return y * sub_ref[...] * (1.0 - lam_init)


def _diff_kernel(q_ref, kc_ref, vc_ref, *rest, tk, n_lat, lam_init, nstream):
    if n_lat:
        k_ref, v_ref, lam_ref, sub_ref, o_ref, vx_ref, sa_ref, sb_ref, vcx_ref, m_ref, acc_ref = rest
    else:
        lam_ref, sub_ref, o_ref, vcx_ref, m_ref, acc_ref = rest
    tq = q_ref.shape[1] // nstream
    lo = lax.broadcasted_iota(jnp.int32, (tq, LANES), 1) < DIFF_HEAD_DIM
    qs = jnp.concatenate([_stack_heads(q_ref[0, s * tq:(s + 1) * tq, :], lo) for s in range(nstream)], axis=0)
    nrow = qs.shape[0]
    halves = [slice(j * nrow // 2, (j + 1) * nrow // 2) for j in range(2)]

    @pl.when(pl.program_id(2) == 0)
    def _():
        vcx_ref[:, :LANES] = vc_ref[0]
        vcx_ref[:, LANES:] = jnp.ones((vcx_ref.shape[0], LANES), BF16)
        if n_lat:
            vx_ref[:, :LANES] = v_ref[0]
            vx_ref[:, LANES:] = jnp.ones((vx_ref.shape[0], LANES), BF16)

    def absorb(s, vxb, rows, first=False):
        m_old = m_ref[rows, :]
        m_new = s.max(axis=-1, keepdims=True) if first else jnp.maximum(m_old, s.max(axis=-1, keepdims=True))
        pv = _dot(jnp.exp2((s - m_new).astype(BF16)), vxb)
        acc_ref[rows, :] = pv if first else jnp.exp2(m_old - m_new) * acc_ref[rows, :] + pv
        m_ref[rows, :] = m_new

    s_ctx = _dot_nt(qs, kc_ref[0])
    if not n_lat:
        absorb(s_ctx, vcx_ref[...], slice(0, nrow), first=True)
    else:
        def lat_k(c):
            return k_ref[0, pl.ds(pl.multiple_of(c * tk, tk), tk), :]

        def lat_vx(c):
            return vx_ref[pl.ds(pl.multiple_of(c * tk, tk), tk), :]

        def phase(cur_ref, nxt_ref, c, prefetch):
            if prefetch:
                nxt_ref[...] = _dot_nt(qs, lat_k(c + 1))
            for rows in halves:
                absorb(cur_ref[rows, :], lat_vx(c), rows)

        sa_ref[...] = _dot_nt(qs, lat_k(0))
        absorb(s_ctx, vcx_ref[...], slice(0, nrow), first=True)

        def body(j, carry):
            phase(sa_ref, sb_ref, 2 * j, True)
            phase(sb_ref, sa_ref, 2 * j + 1, True)
            return carry

        lax.fori_loop(0, n_lat // 2 - 1, body, 0)
        phase(sa_ref, sb_ref, n_lat - 2, True)
        phase(sb_ref, sa_ref, n_lat - 1, False)
    lam_full = _lam_full(lam_ref, lam_init)
    for s in range(nstream):
        acc = acc_ref[2 * s * tq:2 * (s + 1) * tq, :]
        o = acc[:, :LANES] / acc[:, LANES:LANES + 1]
        o = o[:tq] - lam_full * o[tq:]
        o_ref[0, s * tq:(s + 1) * tq, :] = _diff_post(o, sub_ref, lam_init).astype(o_ref.dtype)


def _diff_attention(q_src, qkv_c, qkv, lam, subln, lam_init, tq, tk, nstream=1):
    b, tqs, _ = q_src.shape
    ctx = qkv_c.shape[1]
    h0 = 3 * NA_HEADS // 2
    nh = DIFF_HEADS
    in_specs = [
        pl.BlockSpec((1, tq, LANES), lambda bi, h, i: (bi, i, h0 + h)),
        pl.BlockSpec((1, ctx, LANES), lambda bi, h, i: (bi, 0, h0 + nh + h)),
        pl.BlockSpec((1, ctx, LANES), lambda bi, h, i: (bi, 0, h0 + 2 * nh + h)),
    ]
    args = [q_src, qkv_c, qkv_c]
    n_lat = 0
    scratch = []
    if qkv is not None:
        t = qkv.shape[1]
        n_lat = t // tk
        in_specs += [
            pl.BlockSpec((1, t, LANES), lambda bi, h, i: (bi, 0, h0 + nh + h)),
            pl.BlockSpec((1, t, LANES), lambda bi, h, i: (bi, 0, h0 + 2 * nh + h)),
        ]
        args += [qkv, qkv]
        assert n_lat % 2 == 0
        scratch += [pltpu.VMEM((t, 2 * LANES), BF16), pltpu.VMEM((2 * tq, tk), F32), pltpu.VMEM((2 * tq, tk), F32)]
    scratch += [pltpu.VMEM((ctx, 2 * LANES), BF16), pltpu.VMEM((2 * tq, 1), F32), pltpu.VMEM((2 * tq, 2 * LANES), F32)]
    in_specs += [
        pl.BlockSpec(lam.shape, lambda bi, h, i: (0, 0)),
        pl.BlockSpec((1, LANES), lambda bi, h, i: (0, 0)),
    ]
    args += [lam.astype(F32), subln.astype(F32).reshape(1, LANES)]
    return pl.pallas_call(
        functools.partial(_diff_kernel, tk=tk, n_lat=n_lat, lam_init=lam_init, nstream=nstream),
        grid=(b, nh, tqs // tq),
        in_specs=in_specs,
        out_specs=pl.BlockSpec((1, tq, LANES), lambda bi, h, i: (bi, i, h)),
        out_shape=jax.ShapeDtypeStruct((b, tqs, nh * LANES), BF16),
        scratch_shapes=scratch,
        compiler_params=_params("parallel", "parallel", "arbitrary"),
        name="diff_attention",
    )(*args)


def _chunk_masks(n):
    row = lax.broadcasted_iota(jnp.int32, (n, n), 0)
    col = lax.broadcasted_iota(jnp.int32, (n, n), 1)
    shift = int(math.log2(CHUNK))
    same = lax.shift_right_logical(row, shift) == lax.shift_right_logical(col, shift)
    return row, col, same


def _chunk_scan(x, reverse, pos):
    n, w = x.shape
    s = 1
    while s < 8:
        if reverse:
            x = x + jnp.where(pos < CHUNK - s, pltpu.roll(x, n - s, 0), 0.0)
        else:
            x = x + jnp.where(pos >= s, pltpu.roll(x, s, 0), 0.0)
        s *= 2
    while s < CHUNK:
        parts = []
        for c in range(n // CHUNK):
            xc = x[c * CHUNK:(c + 1) * CHUNK]
            pad = jnp.zeros((s, w), F32)
            parts.append(xc + (jnp.concatenate([xc[s:], pad], axis=0) if reverse
                               else jnp.concatenate([pad, xc[:CHUNK - s]], axis=0)))
        x = jnp.concatenate(parts, axis=0)
        s *= 2
    return x


def _chunk_totals(G, reverse):
    n, w = G.shape
    r8 = lax.broadcasted_iota(jnp.int32, (8, w), 0)
    parts, dec = [], jnp.zeros((8, w), F32)
    for c in range(n // CHUNK):
        r = c * CHUNK + (0 if reverse else CHUNK - 1)
        tot = G[r:r + 1, :]
        parts.append(jnp.broadcast_to(tot, (CHUNK, w)))
        dec = jnp.where(r8 == c, jnp.broadcast_to(tot, (8, w)), dec)
    return jnp.concatenate(parts, axis=0), dec


def _rec_prep_kernel(dq_ref, dk_ref, dv_ref, pq_ref, pk_ref, pv_ref, nq_ref, nk_ref, nv_ref,
                     hq_ref, hi_ref, hff_ref, hfb_ref, dab_ref, cwq_ref, cwk_ref, cwv_ref, lb_ref, ab_ref,
                     cqe_ref, ckt_ref, coi_ref, cdec_ref, cv_ref,
                     dw_ref, du_ref, dkt_ref, dqe_ref, daqk_ref, ddec_ref, *, nblk):
    i = pl.program_id(1)
    n = REC_BLOCK
    w = REC_W
    row, col, same = _chunk_masks(PAIR)
    incl = (same & (row >= col), same & (row <= col))
    strict = (same & (row > col), same & (row < col))
    rid = lax.broadcasted_iota(jnp.int32, (n, w), 0)
    pos = rid & (CHUNK - 1)
    pos1 = pos[:, :LANES]
    pairs = [slice(p * PAIR, (p + 1) * PAIR) for p in range(n // PAIR)]
    heads = [slice(h * LANES, (h + 1) * LANES) for h in range(REC_HEADS)]
    keys = [(h, d, p) for h in range(REC_HEADS) for d in range(2) for p in range(len(pairs))]
    hp_keys = [(h, p) for h in range(REC_HEADS) for p in range(len(pairs))]

    cq = _silu(hq_ref[0].astype(F32)) * (REC_HEAD_DIM ** -0.5)
    cvb = hi_ref[0]
    cv_ref[0] = cvb
    qin, ktail = {}, {}
    for d, f_ref in enumerate((hff_ref, hfb_ref)):
        lbv = lb_ref[pl.ds(d, 1), :]
        sig = _sigmoid(f_ref[0])
        g = jnp.log(lbv + (1.0 - lbv) * sig) * LOG2E
        ck = (1.0 - lbv) * (1.0 - sig)
        G = _chunk_scan(g, d == 1, pos)
        Gl, dec = _chunk_totals(G, d == 1)
        ktail[d] = (ck * jnp.exp2(Gl - G)).astype(BF16)
        qin[d] = (cq * jnp.exp2(G - Gl)).astype(BF16)
        cqe_ref[d, 0] = (cq * jnp.exp2(G)).astype(BF16)
        ckt_ref[d, 0] = ktail[d]
        cdec_ref[d, 0, 0] = jnp.exp2(dec)
    amat = {(h, d, p): jnp.where(incl[d], _dot_nt(qin[d][pairs[p], heads[h]], ktail[d][pairs[p], heads[h]]), 0.0)
            for h, d, p in keys}
    for h, d, p in keys:
        coi_ref[d, 0, pairs[p], heads[h]] = _dot(amat[h, d, p].astype(BF16), cvb[pairs[p], heads[h]])

    last = BF16_ROWS - 1

    def conv_silu(x_ref, p_ref, n_ref, cw_ref):
        x = x_ref[0].astype(F32)
        cw = cw_ref[...]
        prev_row = jnp.where(i > 0, p_ref[0, last:last + 1, :].astype(F32), 0.0)
        next_row = jnp.where(i < nblk - 1, n_ref[0, 0:1, :].astype(F32), 0.0)
        xp = jnp.where(rid == 0, prev_row, pltpu.roll(x, 1, 0))
        xn = jnp.where(rid == n - 1, next_row, pltpu.roll(x, n - 1, 0))
        return _silu(cw[0:1] * xp + cw[1:2] * x + cw[2:3] * xn)

    def l2n(x):
        return jnp.concatenate(
            [x[:, hs] * lax.rsqrt(jnp.sum(x[:, hs] * x[:, hs], axis=-1, keepdims=True) + EPS) for hs in heads], axis=1)

    qn = l2n(conv_silu(dq_ref, pq_ref, nq_ref, cwq_ref)) * (REC_HEAD_DIM ** -0.5)
    kn = l2n(conv_silu(dk_ref, pk_ref, nk_ref, cwk_ref))
    vv = conv_silu(dv_ref, pv_ref, nv_ref, cwv_ref)
    knb, qnb = kn.astype(BF16), qn.astype(BF16)
    kk = {(h, p): _dot_nt(knb[pairs[p], heads[h]], knb[pairs[p], heads[h]]) for h, p in hp_keys}
    qk = {(h, p): _dot_nt(qnb[pairs[p], heads[h]], knb[pairs[p], heads[h]]) for h, p in hp_keys}
    tile = dab_ref[0]
    arate, dtb = ab_ref[0:1, :], ab_ref[1:2, :]
    xs = tile + dtb
    gt = -arate * (jnp.maximum(xs, 0.0) + jnp.log(1.0 + jnp.exp(-jnp.abs(xs))))
    bt = _sigmoid(tile)
    lane = lax.broadcasted_iota(jnp.int32, (n, LANES), 1)
    spread = lambda t, j0: jnp.concatenate(
        [jnp.broadcast_to(jnp.sum(jnp.where(lane == j0 + h, t, 0.0), axis=-1, keepdims=True), (n, LANES))
         for h in range(REC_HEADS)], axis=1)
    eye = (row == col).astype(F32)
    rhs, gmat = {}, {}
    for d in range(2):
        gsc = _chunk_scan(gt, d == 1, pos1)
        Gc = spread(gsc, d * REC_HEADS)
        bb = spread(bt, (2 + d) * REC_HEADS)
        Gl, dec = _chunk_totals(Gc, d == 1)
        eG = jnp.exp2(Gc)
        rhs[d] = ((kn * bb * eG).astype(BF16), (vv * bb).astype(BF16))
        for h, p in hp_keys:
            gcp = Gc[pairs[p], heads[h]]
            gamma = jnp.exp2(jnp.where(incl[d], gcp - gcp.T, MASK_VALUE))
            gmat[h, d, p] = jnp.where(strict[d], bb[pairs[p], heads[h]] * kk[h, p] * gamma, 0.0)
            daqk_ref[d, 0, pairs[p], heads[h]] = jnp.where(incl[d], qk[h, p] * gamma, 0.0).astype(BF16)
        dkt_ref[d, 0] = (kn * jnp.exp2(Gl - Gc)).astype(BF16)
        dqe_ref[d, 0] = (qn * eG).astype(BF16)
        ddec_ref[d, 0, 0] = jnp.exp2(dec)

    T = {key: eye for key in keys}
    for k in range(int(math.log2(CHUNK))):
        blk = lax.shift_right_logical(row, k + 1) == lax.shift_right_logical(col, k + 1)
        sub = lax.shift_right_logical(row, k) != lax.shift_right_logical(col, k)
        ak = {key: jnp.where(blk & sub, gmat[key], 0.0) for key in keys}
        if k == 0:
            T = {key: T[key] - ak[key] for key in keys}
            continue
        Tb = {key: T[key].astype(BF16) for key in keys}
        x = {key: _dot(ak[key].astype(BF16), Tb[key]).astype(BF16) for key in keys}
        T = {key: T[key] - _dot(Tb[key], x[key]) for key in keys}
    for h, d, p in keys:
        both = jnp.concatenate([rhs[d][0][pairs[p], heads[h]], rhs[d][1][pairs[p], heads[h]]], axis=1)
        wu = _dot(T[h, d, p].astype(BF16), both)
        dw_ref[d, 0, pairs[p], heads[h]] = wu[:, :LANES].astype(BF16)
        du_ref[d, 0, pairs[p], heads[h]] = wu[:, LANES:].astype(BF16)


def _rec_prep(p16, p32, conv_w, lb, ab):
    b, t, _ = p16.shape
    nblk = t // REC_BLOCK
    n = REC_BLOCK
    w = REC_W
    per = n // BF16_ROWS
    main = lambda blk: pl.BlockSpec((1, n, w), lambda bi, i: (bi, i, blk))
    prev = lambda blk: pl.BlockSpec((1, BF16_ROWS, w), lambda bi, i: (bi, jnp.maximum(i * per - 1, 0), blk))
    nxt = lambda blk: pl.BlockSpec((1, BF16_ROWS, w),
                                   lambda bi, i: (bi, jnp.minimum((i + 1) * per, t // BF16_ROWS - 1), blk))
    taps = lambda blk: pl.BlockSpec((8, w), lambda bi, i: (0, blk))
    in_specs = [main(P16_DQ), main(P16_DK), main(P16_DV), prev(P16_DQ), prev(P16_DK), prev(P16_DV),
                nxt(P16_DQ), nxt(P16_DK), nxt(P16_DV), main(P16_HQ), main(P16_HI),
                main(P32_HFF), main(P32_HFB),
                pl.BlockSpec((1, n, LANES), lambda bi, i: (bi, i, P32_DAB_TILE)),
                taps(0), taps(1), taps(2),
                pl.BlockSpec((8, w), lambda bi, i: (0, 0)),
                pl.BlockSpec((8, LANES), lambda bi, i: (0, 0))]
    dir_tok = lambda dt: (jax.ShapeDtypeStruct((2, b, t, w), dt),
                          pl.BlockSpec((2, 1, n, w), lambda bi, i: (0, bi, i, 0)))
    dir_dec = (jax.ShapeDtypeStruct((2, b, nblk, 8, w), F32),
               pl.BlockSpec((2, 1, 1, 8, w), lambda bi, i: (0, bi, i, 0, 0)))
    outs = [dir_tok(BF16), dir_tok(BF16), dir_tok(F32), dir_dec,
            (jax.ShapeDtypeStruct((b, t, w), BF16), pl.BlockSpec((1, n, w), lambda bi, i: (bi, i, 0))),
            dir_tok(BF16), dir_tok(BF16), dir_tok(BF16), dir_tok(BF16), dir_tok(BF16), dir_dec]
    return pl.pallas_call(
        functools.partial(_rec_prep_kernel, nblk=nblk),
        grid=(b, nblk),
        in_specs=in_specs,
        out_specs=[o[1] for o in outs],
        out_shape=[o[0] for o in outs],
        compiler_params=_params("parallel", "arbitrary"),
        name="rec_prep",
    )(*([p16] * 11), p32, p32, p32, conv_w, conv_w, conv_w, lb, ab)


_N_PREP = 11


def _rec_scan_kernel(*refs, nblk, nb, has_init, with_output):
    ins = [refs[d * _N_PREP:(d + 1) * _N_PREP] for d in range(2)]
    rest = list(refs[2 * _N_PREP:])
    s0c_ref, s0d_ref = (rest.pop(0), rest.pop(0)) if has_init else (None, None)
    if with_output:
        oc_refs, od_refs = (rest.pop(0), rest.pop(0)), (rest.pop(0), rest.pop(0))
    sfc_ref, sfd_ref, sc_ref, sd_ref = rest
    i = pl.program_id(1)

    @pl.when(i == 0)
    def _():
        if has_init:
            sc_ref[...] = s0c_ref[...]
            sd_ref[...] = s0d_ref[...]
        else:
            sc_ref[...] = jnp.zeros_like(sc_ref)
            sd_ref[...] = jnp.zeros_like(sd_ref)

    nchunk = REC_BLOCK // CHUNK
    keys = [(e, d, h) for e in range(nb) for d in range(2) for h in range(REC_HEADS)]
    for step in range(nchunk):
        chunk_of = lambda d: step if d == 0 else nchunk - 1 - step

        def tok(k, idx):
            e, d, h = k
            c = chunk_of(d)
            ref = ins[d][idx]
            sl = (e, slice(c * CHUNK, (c + 1) * CHUNK), slice(h * LANES, (h + 1) * LANES))
            return ref[(0,) * (len(ref.shape) - 3) + sl]

        def dec(k, idx):
            e, d, h = k
            c = chunk_of(d)
            return ins[d][idx][0, e, 0, c:c + 1, h * LANES:(h + 1) * LANES]

        def out_slice(k):
            e, d, h = k
            c = chunk_of(d)
            return (e, slice(c * CHUNK, (c + 1) * CHUNK), slice(h * LANES, (h + 1) * LANES))

        S = {k: sd_ref[k] for k in keys}
        Sb = {k: S[k].astype(BF16) for k in keys}
        if with_output:
            both = {k: _dot(jnp.concatenate([tok(k, 5), tok(k, 8)], axis=0), Sb[k]) for k in keys}
            ws = {k: both[k][:CHUNK] for k in keys}
            qs = {k: both[k][CHUNK:] for k in keys}
        else:
            ws = {k: _dot(tok(k, 5), Sb[k]) for k in keys}
        ST = {k: sc_ref[k] for k in keys}
        if with_output:
            for k in keys:
                oc_refs[k[1]][out_slice(k)] = (tok(k, 2) + _dot_nt(tok(k, 0), ST[k].astype(BF16))).astype(BF16)
        for k in keys:
            sc_ref[k] = ST[k] * dec(k, 3) + _dot_tn(tok(k, 4), tok(k, 1))
        vnb = {k: (tok(k, 6).astype(F32) - ws[k]).astype(BF16) for k in keys}
        for k in keys:
            sd_ref[k] = S[k] * dec(k, 10) + _dot_tn(tok(k, 7), vnb[k])
        if with_output:
            for k in keys:
                vpair = jnp.concatenate([vnb[k], vnb[k]], axis=0)
                od_refs[k[1]][out_slice(k)] = (qs[k] + _dot(tok(k, 9), vpair)).astype(BF16)

    @pl.when(i == nblk - 1)
    def _():
        sfc_ref[...] = sc_ref[...]
        sfd_ref[...] = sd_ref[...]


def _rec_scan(prep, init, with_output):
    _, b, t, w = prep[0].shape
    nblk = t // REC_BLOCK
    n = REC_BLOCK
    nb = 2 if b % 2 == 0 else 1
    in_specs, args = [], []
    for d in range(2):
        blk = (lambda i: i) if d == 0 else (lambda i: nblk - 1 - i)
        tok = pl.BlockSpec((1, nb, n, w), lambda bi, i, d=d, blk=blk: (d, bi, blk(i), 0))
        dec = pl.BlockSpec((1, nb, 1, 8, w), lambda bi, i, d=d, blk=blk: (d, bi, blk(i), 0, 0))
        cvs = pl.BlockSpec((nb, n, w), lambda bi, i, blk=blk: (bi, blk(i), 0))
        in_specs += [tok, tok, tok, dec, cvs, tok, tok, tok, tok, tok, dec]
        args += list(prep)
    state = pl.BlockSpec((nb, 2, REC_HEADS, LANES, LANES), lambda bi, i: (bi, 0, 0, 0, 0))
    if init is not None:
        in_specs += [state, state]
        args += list(init)
    out_specs, out_shape = [], []
    if with_output:
        fwd = pl.BlockSpec((nb, n, w), lambda bi, i: (bi, i, 0))
        bwd = pl.BlockSpec((nb, n, w), lambda bi, i: (bi, nblk - 1 - i, 0))
        out_specs += [fwd, bwd, fwd, bwd]
        out_shape += [jax.ShapeDtypeStruct((b, t, w), BF16)] * 4
    out_specs += [state, state]
    out_shape += [jax.ShapeDtypeStruct((b, 2, REC_HEADS, LANES, LANES), F32)] * 2
    res = pl.pallas_call(
        functools.partial(_rec_scan_kernel, nblk=nblk, nb=nb, has_init=init is not None, with_output=with_output),
        grid=(b // nb, nblk),
        in_specs=in_specs,
        out_specs=out_specs,
        out_shape=out_shape,
        scratch_shapes=[pltpu.VMEM((nb, 2, REC_HEADS, LANES, LANES), F32)] * 2,
        compiler_params=_params("parallel", "arbitrary"),
        name="rec_scan",
    )(*args)
    return (res[:4], (res[4], res[5])) if with_output else (None, (res[0], res[1]))


def _gated_head_norm(f_ref, b_ref, gate_ref, nw_ref):
    o = f_ref[0].astype(F32) + b_ref[0].astype(F32)
    gate = _silu(gate_ref[0].astype(F32))
    parts = []
    for h in range(REC_HEADS):
        oh = o[:, h * LANES:(h + 1) * LANES]
        parts.append(oh * lax.rsqrt(jnp.mean(oh * oh, axis=-1, keepdims=True) + EPS) * nw_ref[...])
    return (jnp.concatenate(parts, axis=1) * gate).astype(BF16)


def _tail_kernel(*refs, recurrent, final, f_splits):
    refs = list(refs)
    x_ref = refs.pop(0)
    if recurrent:
        ocf, ocb, odf, odb, cg, dg, cn, dn = [refs.pop(0) for _ in range(8)]
        ya, yb = _gated_head_norm(ocf, ocb, cg, cn), _gated_head_norm(odf, odb, dg, dn)
    else:
        ya, yb = refs.pop(0)[0], refs.pop(0)[0]
    wa_ref, wb_ref, gm_ref, nw_ref, sh_ref, sc_ref, gf_ref, wg_ref, wu_ref, wd_ref = refs[:10]
    o_ref = refs[-1]
    x1 = x_ref[0] + gm_ref[0] * (_dot(ya, wa_ref[...]) + _dot(yb, wb_ref[...]))
    h = _rms_mod(x1, nw_ref[...], sh_ref[0], sc_ref[0]).astype(BF16)
    acc = None
    for f0, f1 in f_splits:
        z = (_silu(_dot(h, wg_ref[:, f0:f1])) * _dot(h, wu_ref[:, f0:f1])).astype(BF16)
        part = _dot(z, wd_ref[f0:f1, :])
        acc = part if acc is None else acc + part
    out = x1 + gf_ref[0] * acc
    if final:
        out = out * lax.rsqrt(jnp.mean(out * out, axis=-1, keepdims=True) + EPS) * refs[10][...]
    o_ref[0] = out


def _tail(x, mix_inputs, wa, wb, g_mix, nw, shift, scale, g_ffn, wg, wu, wd, final_norm, tm, norms=None):
    b, t, d = x.shape
    f = wg.shape[1]
    recurrent = norms is not None
    tokw = lambda width, blk=0: pl.BlockSpec((1, tm, width), lambda bi, i: (bi, i, blk))
    vec = lambda width: pl.BlockSpec((1, width), lambda bi, i: (0, 0))
    in_specs = [tokw(d)]
    args = [x]
    if recurrent:
        *outs, p16 = mix_inputs
        in_specs += [tokw(REC_W)] * 4 + [tokw(REC_W, P16_HGATE), tokw(REC_W, P16_DGATE), vec(LANES), vec(LANES)]
        args += list(outs) + [p16, p16, norms[0].reshape(1, LANES), norms[1].reshape(1, LANES)]
    else:
        in_specs += [tokw(a.shape[2]) for a in mix_inputs]
        args += list(mix_inputs)
    in_specs += [_resident(wa.shape), _resident(wb.shape), _mod_spec(g_mix, d), vec(d),
                 _mod_spec(shift, d), _mod_spec(scale, d), _mod_spec(g_ffn, d),
                 _resident((d, f)), _resident((d, f)), _resident((f, d))]
    args += [wa, wb, g_mix, nw.reshape(1, d), shift, scale, g_ffn, wg, wu, wd]
    if final_norm is not None:
        in_specs.append(vec(d))
        args.append(final_norm.reshape(1, d))
    mxu = 2 * LANES
    f_mid = -(-f // (2 * mxu)) * mxu
    return pl.pallas_call(
        functools.partial(_tail_kernel, recurrent=recurrent, final=final_norm is not None,
                          f_splits=((0, f_mid), (f_mid, f))),
        grid=(b, t // tm),
        in_specs=in_specs,
        out_specs=pl.BlockSpec((1, tm, d), lambda bi, i: (bi, i, 0)),
        out_shape=jax.ShapeDtypeStruct((b, t, d), F32),
        compiler_params=_params("parallel", "parallel"),
        name="layer_tail",
    )(*args)


def _pad_rows(a, rows):
    return jnp.pad(a, ((0, rows - a.shape[0]),) + ((0, 0),) * (a.ndim - 1))


def kernel(x, c, ctx, c_ctx, ada_w, ada_b, norm_mix, norm_ffn, w_mix_out, ffn_gate, ffn_up, ffn_down,
           att_w_in, att_rpb, att_lambda, att_subln, rec_w_in, rec_lb_logits, rec_conv_w, rec_a_log,
           rec_dt_bias, rec_c_norm, rec_d_norm, final_norm):
    b, t, d = x.shape
    n_ctx = ctx.shape[1]
    depth = ada_w.shape[0]
    assert depth == 2 and t % (8 * GRID_W) == 0 and n_ctx % REC_BLOCK == 0
    tm = 512 if t % 512 == 0 else 256
    tmp = 1024 if t % 1024 == 0 else tm
    tmc = 256

    nrow = -(-(b + 1) // 8) * 8
    mods = _adaln(_pad_rows(jnp.concatenate([c, c_ctx[None, :]], axis=0).astype(F32), nrow), ada_w, ada_b)

    def layer_mods(layer):
        m = mods[layer].reshape(nrow, 6, 1, d)
        return [m[:b, k] for k in range(6)], [m[b:b + 1, k] for k in range(6)]

    bf = lambda a: a.astype(BF16)
    h_ctx = ctx

    ml, mc = layer_mods(0)
    na_w, df_w = NA_HEADS * NA_HEAD_DIM, 2 * DIFF_HEADS * DIFF_HEAD_DIM
    qscale = jnp.ones((att_w_in.shape[2],), F32)
    qscale = qscale.at[:na_w].set(NA_HEAD_DIM ** -0.5 * LOG2E).at[3 * na_w:3 * na_w + df_w].set(DIFF_HEAD_DIM ** -0.5 * LOG2E)
    w_in = bf(att_w_in[0] * qscale[None, :])
    rope = _rope_tables(t)
    qkv = _proj(x, norm_mix[0], ml[0], ml[1], w_in, BF16, tmp, 512, rope=rope, rope_tiles=(3, 4))
    qkv_c = _proj(h_ctx, norm_mix[0], mc[0], mc[1], w_in, BF16, tmc, 512)
    lam_init = 0.8 - 0.6 * math.exp(-0.3 * 0)
    tb = _na_bias_table(att_rpb[0])
    y_na = _na_attention(qkv, qkv_c, tb)
    y_df = _diff_attention(qkv, qkv_c, qkv, att_lambda[0], att_subln[0], lam_init, tq=1024, tk=min(2048, t // 2), nstream=4)
    yc_na = _ctx_na_attention(qkv_c)
    yc_df = _diff_attention(qkv_c, qkv_c, None, att_lambda[0], att_subln[0], lam_init, tq=n_ctx, tk=0)
    wo = bf(w_mix_out[0])
    wg, wu, wd = bf(ffn_gate[0]), bf(ffn_up[0]), bf(ffn_down[0])
    x = _tail(x, (y_na, y_df), wo[:na_w], wo[na_w:], ml[2], norm_ffn[0], ml[3], ml[4], ml[5], wg, wu, wd, None, tm)
    h_ctx = _tail(h_ctx, (yc_na, yc_df), wo[:na_w], wo[na_w:], mc[2], norm_ffn[0], mc[3], mc[4], mc[5], wg, wu, wd,
                  None, tmc)

    ml, mc = layer_mods(1)
    hw = REC_W
    wr = rec_w_in[0]
    cuts = [0, hw, 2 * hw, 3 * hw, 4 * hw, 5 * hw, 8 * hw, 8 * hw + 4 * REC_HEADS, 9 * hw + 4 * REC_HEADS]
    hq, hff, hfb, hi, hgate, dqkv, dab, dgate = [wr[:, cuts[k]:cuts[k + 1]] for k in range(8)]
    w16 = bf(jnp.concatenate([dqkv, hq, hi, hgate, dgate], axis=1))
    w32 = bf(jnp.concatenate([hff, hfb, jnp.pad(dab, ((0, 0), (0, LANES - dab.shape[1])))], axis=1))
    lbs = jax.nn.softmax(rec_lb_logits.astype(F32), axis=0)
    lb = _pad_rows((jnp.cumsum(lbs, axis=0) - lbs[0])[1], 8)
    conv_w = _pad_rows(rec_conv_w[0].astype(F32), 8)
    a_rate = (jnp.exp(rec_a_log[0].astype(F32)) * LOG2E).reshape(-1)
    dtb = rec_dt_bias[0].astype(F32).reshape(-1)
    ab = jnp.zeros((8, LANES), F32).at[0, :a_rate.shape[0]].set(a_rate).at[1, :dtb.shape[0]].set(dtb)

    def rec_inputs(h, mods_, tile):
        p16 = _proj(h, norm_mix[1], mods_[0], mods_[1], w16, BF16, tile, 512)
        p32 = _proj(h, norm_mix[1], mods_[0], mods_[1], w32, F32, tile, w32.shape[1])
        return p16, _rec_prep(p16, p32, conv_w, lb, ab)

    _, prep_c = rec_inputs(h_ctx, mc, tmc)
    p16, prep_l = rec_inputs(x, ml, tmp)
    _, s_ctx = _rec_scan(prep_c, None, with_output=False)
    outs, _ = _rec_scan(prep_l, s_ctx, with_output=True)
    wo = bf(w_mix_out[1])
    return _tail(x, (*outs, p16), wo[:hw], wo[hw:], ml[2], norm_ffn[1], ml[3], ml[4], ml[5],
                 bf(ffn_gate[1]), bf(ffn_up[1]), bf(ffn_down[1]), final_norm, tm,
                 norms=(rec_c_norm[0].astype(F32), rec_d_norm[0].astype(F32)))
```
